```python
import math
import jax
import jax.numpy as jnp
from jax import lax
import numpy as np


D_MODEL = 2048
BATCH = 16
SEQ = 2048
DEPTH = 2

MEM_LEN = 256
Q_BLOCK = 128
ROPE_THETA = 10000.0
NORM_EPS = 1e-6

DIFF_HEADS = 12
DIFF_HEAD_DIM = 64
DIFF_WIDTH = DIFF_HEADS * 2 * DIFF_HEAD_DIM
S5_WIDTH = D_MODEL // 4
S5_GROUP = 16
S5_GROUPS = S5_WIDTH // S5_GROUP
S5_STATE = 64
AB_IN_WIDTH = 3 * DIFF_WIDTH + S5_WIDTH
AB_OUT_WIDTH = DIFF_WIDTH + S5_WIDTH
MLA_HEADS = 16
MLA_Q_RANK = 768
MLA_KV_RANK = 256
MLA_NOPE = 128
MLA_ROPE = 64
MLA_V = 128
MLA_IN_WIDTH = MLA_Q_RANK + MLA_KV_RANK + MLA_ROPE
XA_HEADS = 4
XA_HEAD_DIM = D_MODEL // XA_HEADS
PEER_HEADS = 8
PEER_N_KEYS = 128
PEER_N_EXPERTS = PEER_N_KEYS * PEER_N_KEYS
PEER_TOPK = 16
PEER_KEY_DIM = 256
PEER_HALF = PEER_KEY_DIM // 2
PEER_TOKEN_CHUNK = 128

N_EVEN = (DEPTH + 1) // 2
N_ODD = DEPTH // 2

kernel_name = 'hybrid_diffattn_s5_mla_peer'


def rmsnorm(x, g):
    xf = x.astype(jnp.float32)
    y = xf * lax.rsqrt(jnp.mean(xf * xf, axis=-1, keepdims=True) + NORM_EPS)
    return (y * g.astype(jnp.float32)).astype(x.dtype)


def rope(x, positions):
    d = x.shape[-1]
    half = d // 2
    inv_freq = ROPE_THETA ** (-jnp.arange(half, dtype=jnp.float32) * 2.0 / d)
    ang = positions.astype(jnp.float32)[..., None] * inv_freq
    ang = ang.reshape((ang.shape[0],) + (1,) * (x.ndim - 3) + (ang.shape[1], half))
    cos, sin = jnp.cos(ang), jnp.sin(ang)
    xf = x.astype(jnp.float32)
    x1, x2 = xf[..., :half], xf[..., half:]
    return jnp.concatenate([x1 * cos - x2 * sin, x2 * cos + x1 * sin], axis=-1).astype(x.dtype)


def causal_attention(q, k, v, scale):
    seq = q.shape[2]
    outs = []
    for start in range(0, seq, Q_BLOCK):
        end = start + Q_BLOCK
        s = jnp.einsum('bhqd,bhkd->bhqk', q[:, :, start:end], k[:, :, :end]).astype(jnp.float32) * scale
        mask = (start + jnp.arange(Q_BLOCK))[:, None] >= jnp.arange(end)[None, :]
        p = jax.nn.softmax(jnp.where(mask, s, -jnp.inf), axis=-1)
        outs.append(jnp.einsum('bhqk,bhkd->bhqd', p.astype(v.dtype), v[:, :, :end]))
    return jnp.concatenate(outs, axis=2)


def diff_attention(q, k, v, positions, lq1, lk1, lq2, lk2, subln, lam_init):
    b, s, _ = q.shape
    q = rope(q.reshape(b, s, DIFF_HEADS, 2, DIFF_HEAD_DIM).transpose(0, 2, 3, 1, 4), positions)
    k = rope(k.reshape(b, s, DIFF_HEADS, 2, DIFF_HEAD_DIM).transpose(0, 2, 3, 1, 4), positions)
    v = v.reshape(b, s, DIFF_HEADS, 2 * DIFF_HEAD_DIM).transpose(0, 2, 1, 3)
    lam = (jnp.exp(jnp.sum(lq1.astype(jnp.float32) * lk1.astype(jnp.float32)))
           - jnp.exp(jnp.sum(lq2.astype(jnp.float32) * lk2.astype(jnp.float32))) + lam_init)
    scale = DIFF_HEAD_DIM ** -0.5
    o1 = causal_attention(q[:, :, 0], k[:, :, 0], v, scale)
    o2 = causal_attention(q[:, :, 1], k[:, :, 1], v, scale)
    o = (o1.astype(jnp.float32) - lam * o2.astype(jnp.float32)).astype(v.dtype)
    o = rmsnorm(o, subln) * (1.0 - lam_init)
    return o.transpose(0, 2, 1, 3).reshape(b, s, DIFF_WIDTH)


def s5_mixer(u, a_re, a_im, log_step, b_re, b_im, c_re, c_im, d, w_glu, b_glu):
    bsz, s, _ = u.shape
    f32 = jnp.float32
    uf = u.astype(f32).reshape(bsz, s, S5_GROUPS, S5_GROUP)
    a_re = a_re.astype(f32)
    a_im = a_im.astype(f32)
    step = jnp.exp(log_step.astype(f32))[:, None]
    mag = jnp.exp(a_re * step)
    ab_re = mag * jnp.cos(a_im * step)
    ab_im = mag * jnp.sin(a_im * step)
    den = a_re * a_re + a_im * a_im
    num_re = ab_re - 1.0
    f_re = (num_re * a_re + ab_im * a_im) / den
    f_im = (ab_im * a_re - num_re * a_im) / den
    b_re = b_re.astype(f32)
    b_im = b_im.astype(f32)
    bb_re = f_re[..., None] * b_re - f_im[..., None] * b_im
    bb_im = f_re[..., None] * b_im + f_im[..., None] * b_re
    bu_re = jnp.einsum('bsgp,gnp->bsgn', uf, bb_re)
    bu_im = jnp.einsum('bsgp,gnp->bsgn', uf, bb_im)

    def combine(left, right):
        al_re, al_im, bl_re, bl_im = left
        ar_re, ar_im, br_re, br_im = right
        return (ar_re * al_re - ar_im * al_im,
                ar_re * al_im + ar_im * al_re,
                ar_re * bl_re - ar_im * bl_im + br_re,
                ar_re * bl_im + ar_im * bl_re + br_im)

    a_seq_re = jnp.broadcast_to(ab_re[None, None], (1, s, S5_GROUPS, S5_STATE))
    a_seq_im = jnp.broadcast_to(ab_im[None, None], (1, s, S5_GROUPS, S5_STATE))
    _, _, h_re, h_im = lax.associative_scan(combine, (a_seq_re, a_seq_im, bu_re, bu_im), axis=1)
    y = (jnp.einsum('bsgn,gpn->bsgp', h_re, c_re.astype(f32))
         - jnp.einsum('bsgn,gpn->bsgp', h_im, c_im.astype(f32))
         + d.astype(f32) * uf).reshape(bsz, s, S5_WIDTH)
    g = jax.nn.gelu(y, approximate=False)
    out = g * jax.nn.sigmoid(g @ w_glu.astype(f32) + b_glu.astype(f32))
    return out.astype(u.dtype)


def mla(h, positions, w_in, q_norm, kv_norm, w_uq, w_ukv, w_o):
    b, s, _ = h.shape
    proj = h @ w_in
    cq = proj[..., :MLA_Q_RANK]
    ckv = proj[..., MLA_Q_RANK:MLA_Q_RANK + MLA_KV_RANK]
    k_rope = proj[..., MLA_Q_RANK + MLA_KV_RANK:]
    q = (rmsnorm(cq, q_norm) @ w_uq).reshape(b, s, MLA_HEADS, MLA_NOPE + MLA_ROPE).transpose(0, 2, 1, 3)
    q = jnp.concatenate([q[..., :MLA_NOPE], rope(q[..., MLA_NOPE:], positions)], axis=-1)
    kv = (rmsnorm(ckv, kv_norm) @ w_ukv).reshape(b, s, MLA_HEADS, MLA_NOPE + MLA_V).transpose(0, 2, 1, 3)
    k_rope = rope(k_rope[:, None], positions)
    k = jnp.concatenate([kv[..., :MLA_NOPE], jnp.broadcast_to(k_rope, (b, MLA_HEADS, s, MLA_ROPE))], axis=-1)
    o = causal_attention(q, k, kv[..., MLA_NOPE:], (MLA_NOPE + MLA_ROPE) ** -0.5)
    return o.transpose(0, 2, 1, 3).reshape(b, s, MLA_HEADS * MLA_V) @ w_o


def memory_cross_attention(h, mem_n, w_q, w_kv, w_o):
    b, s, _ = h.shape
    m = mem_n.shape[1]
    q = (h @ w_q).reshape(b, s, XA_HEADS, XA_HEAD_DIM)
    kv = mem_n @ w_kv
    k = kv[..., :D_MODEL].reshape(b, m, XA_HEADS, XA_HEAD_DIM)
    v = kv[..., D_MODEL:].reshape(b, m, XA_HEADS, XA_HEAD_DIM)
    sc = jnp.einsum('bshd,bmhd->bhsm', q, k).astype(jnp.float32) * XA_HEAD_DIM ** -0.5
    p = jax.nn.softmax(sc, axis=-1)
    o = jnp.einsum('bhsm,bmhd->bshd', p.astype(v.dtype), v).reshape(b, s, D_MODEL)
    return o @ w_o


def peer_ffn(h, w_query, sub_keys, u_table, v_table):
    b, s, d = h.shape
    t = h.reshape(b * s, d)
    n_tok = t.shape[0]
    q = (t @ w_query).reshape(n_tok, PEER_HEADS, 2, PEER_HALF)
    sc = jnp.einsum('thcq,hcnq->thcn', q, sub_keys).astype(jnp.float32)
    top_s, top_i = lax.top_k(sc, PEER_TOPK)
    cand_s = top_s[:, :, 0, :, None] + top_s[:, :, 1, None, :]
    cand_i = top_i[:, :, 0, :, None] * PEER_N_KEYS + top_i[:, :, 1, None, :]
    best_s, best_pos = lax.top_k(cand_s.reshape(n_tok, PEER_HEADS, PEER_TOPK * PEER_TOPK), PEER_TOPK)
    expert_idx = jnp.take_along_axis(cand_i.reshape(n_tok, PEER_HEADS, PEER_TOPK * PEER_TOPK), best_pos, axis=-1)
    gates = jax.nn.softmax(best_s, axis=-1)
    n_chunks = n_tok // PEER_TOKEN_CHUNK

    def chunk_fn(args):
        tc, ic, gc = args
        act = jax.nn.gelu(jnp.einsum('cd,chkd->chk', tc, u_table[ic]).astype(jnp.float32), approximate=False)
        w = (gc * act).astype(tc.dtype)
        return jnp.einsum('chk,chkd->cd', w, v_table[ic])

    out = lax.map(chunk_fn, (t.reshape(n_chunks, PEER_TOKEN_CHUNK, d),
                             expert_idx.reshape(n_chunks, PEER_TOKEN_CHUNK, PEER_HEADS, PEER_TOPK),
                             gates.reshape(n_chunks, PEER_TOKEN_CHUNK, PEER_HEADS, PEER_TOPK)))
    return out.reshape(b, s, d)


def setup_inputs(seed: int = 0) -> dict:
    key = jax.random.key(seed)
    ks = iter(jax.random.split(key, 64))

    def nrm(shape, scale):
        return scale * jax.random.normal(next(ks), shape, jnp.float32)

    def gain(shape):
        return 1.0 + nrm(shape, 0.02)

    x = nrm((BATCH, SEQ, D_MODEL), 1.0)
    mem = nrm((BATCH, MEM_LEN, D_MODEL), 1.0)
    offsets = jax.random.randint(next(ks), (BATCH, 1), 0, 1024, dtype=jnp.int32)
    positions = (offsets + jnp.arange(SEQ, dtype=jnp.int32)[None, :]).astype(jnp.int32)
    NE, NO, L = N_EVEN, N_ODD, DEPTH
    G, N, P = S5_GROUPS, S5_STATE, S5_GROUP
    return {
        'x': x, 'mem': mem, 'positions': positions,
        'ab_norm': gain((NE, D_MODEL)),
        'ab_w_in': nrm((NE, D_MODEL, AB_IN_WIDTH), D_MODEL ** -0.5),
        'ab_w_out': nrm((NE, AB_OUT_WIDTH, D_MODEL), AB_OUT_WIDTH ** -0.5),
        'diff_lq1': nrm((NE, DIFF_HEAD_DIM), 0.1),
        'diff_lk1': nrm((NE, DIFF_HEAD_DIM), 0.1),
        'diff_lq2': nrm((NE, DIFF_HEAD_DIM), 0.1),
        'diff_lk2': nrm((NE, DIFF_HEAD_DIM), 0.1),
        'diff_subln': gain((NE, 2 * DIFF_HEAD_DIM)),
        's5_a_re': -0.5 + nrm((NE, G, N), 0.01),
        's5_a_im': jnp.pi * jnp.arange(N, dtype=jnp.float32) + nrm((NE, G, N), 0.01),
        's5_log_step': jax.random.uniform(next(ks), (NE, G), jnp.float32, math.log(1e-3), math.log(1e-1)),
        's5_b_re': nrm((NE, G, N, P), (2 * P) ** -0.5),
        's5_b_im': nrm((NE, G, N, P), (2 * P) ** -0.5),
        's5_c_re': nrm((NE, G, P, N), (2 * N) ** -0.5 * 4.0),
        's5_c_im': nrm((NE, G, P, N), (2 * N) ** -0.5 * 4.0),
        's5_d': nrm((NE, G, P), 1.0),
        's5_w_glu': nrm((NE, S5_WIDTH, S5_WIDTH), S5_WIDTH ** -0.5),
        's5_b_glu': nrm((NE, S5_WIDTH), 0.02),
        'mla_norm': gain((NO, D_MODEL)),
        'mla_w_in': nrm((NO, D_MODEL, MLA_IN_WIDTH), D_MODEL ** -0.5),
        'mla_q_norm': gain((NO, MLA_Q_RANK)),
        'mla_kv_norm': gain((NO, MLA_KV_RANK)),
        'mla_w_uq': nrm((NO, MLA_Q_RANK, MLA_HEADS * (MLA_NOPE + MLA_ROPE)), MLA_Q_RANK ** -0.5),
        'mla_w_ukv': nrm((NO, MLA_KV_RANK, MLA_HEADS * (MLA_NOPE + MLA_V)), MLA_KV_RANK ** -0.5),
        'mla_w_o': nrm((NO, MLA_HEADS * MLA_V, D_MODEL), (MLA_HEADS * MLA_V) ** -0.5),
        'xa_norm': gain((L, D_MODEL)),
        'xa_mem_norm': gain((L, D_MODEL)),
        'xa_w_q': nrm((L, D_MODEL, D_MODEL), D_MODEL ** -0.5),
        'xa_w_kv': nrm((L, D_MODEL, 2 * D_MODEL), D_MODEL ** -0.5),
        'xa_w_o': nrm((L, D_MODEL, D_MODEL), D_MODEL ** -0.5),
        'ffn_norm': gain((L, D_MODEL)),
        'peer_w_query': nrm((L, D_MODEL, PEER_HEADS * PEER_KEY_DIM), D_MODEL ** -0.5),
        'peer_sub_keys': nrm((L, PEER_HEADS, 2, PEER_N_KEYS, PEER_HALF), PEER_HALF ** -0.5),
        'peer_u': nrm((L, PEER_N_EXPERTS, D_MODEL), D_MODEL ** -0.5),
        'peer_v': nrm((L, PEER_N_EXPERTS, D_MODEL), PEER_HEADS ** -0.5),
        'final_norm': gain((D_MODEL,)),
    }


def reference(x, mem, positions, ab_norm, ab_w_in, ab_w_out, diff_lq1, diff_lk1, diff_lq2, diff_lk2,
              diff_subln, s5_a_re, s5_a_im, s5_log_step, s5_b_re, s5_b_im, s5_c_re, s5_c_im, s5_d,
              s5_w_glu, s5_b_glu, mla_norm, mla_w_in, mla_q_norm, mla_kv_norm, mla_w_uq, mla_w_ukv,
              mla_w_o, xa_norm, xa_mem_norm, xa_w_q, xa_w_kv, xa_w_o, ffn_norm, peer_w_query,
              peer_sub_keys, peer_u, peer_v, final_norm):
    for layer in range(DEPTH):
        i = layer // 2
        if layer % 2 == 0:
            h = rmsnorm(x, ab_norm[i])
            proj = h @ ab_w_in[i]
            q = proj[..., :DIFF_WIDTH]
            k = proj[..., DIFF_WIDTH:2 * DIFF_WIDTH]
            v = proj[..., 2 * DIFF_WIDTH:3 * DIFF_WIDTH]
            u = proj[..., 3 * DIFF_WIDTH:]
            lam_init = 0.8 - 0.6 * math.exp(-0.3 * layer)
            a_out = diff_attention(q, k, v, positions, diff_lq1[i], diff_lk1[i], diff_lq2[i],
                                   diff_lk2[i], diff_subln[i], lam_init)
            b_out = s5_mixer(u, s5_a_re[i], s5_a_im[i], s5_log_step[i], s5_b_re[i], s5_b_im[i],
                             s5_c_re[i], s5_c_im[i], s5_d[i], s5_w_glu[i], s5_b_glu[i])
            x = x + jnp.concatenate([a_out, b_out.astype(a_out.dtype)], axis=-1) @ ab_w_out[i]
        else:
            x = x + mla(rmsnorm(x, mla_norm[i]), positions, mla_w_in[i], mla_q_norm[i],
                        mla_kv_norm[i], mla_w_uq[i], mla_w_ukv[i], mla_w_o[i])
        x = x + memory_cross_attention(rmsnorm(x, xa_norm[layer]), rmsnorm(mem, xa_mem_norm[layer]),
                                       xa_w_q[layer], xa_w_kv[layer], xa_w_o[layer])
        x = x + peer_ffn(rmsnorm(x, ffn_norm[layer]), peer_w_query[layer], peer_sub_keys[layer],
                         peer_u[layer], peer_v[layer])
    return rmsnorm(x, final_norm)
```

```python
import functools
import math

import jax
import jax.numpy as jnp
from jax import lax
from jax.experimental import pallas as pl
from jax.experimental.pallas import tpu as pltpu

F32 = jnp.float32
BF16 = jnp.bfloat16

NORM_EPS = 1e-6
ROPE_THETA = 10000.0
LANES = 128
SUBLANES = 8
VMEM_LIMIT = 56 * 1024 * 1024

DIFF_HEADS = 12
DIFF_HEAD_DIM = 64
S5_GROUP = 16
S5_STATE = 64
MLA_HEADS = 16
MLA_Q_RANK = 768
MLA_KV_RANK = 256
MLA_NOPE = 128
MLA_ROPE = 64
MLA_QK_PAD = 256
XA_HEADS = 4
PEER_HEADS = 8
PEER_N_KEYS = 128
PEER_TOPK = 16
PEER_HALF = 128


def _dot(a, b):
    return jnp.dot(a, b, preferred_element_type=F32)


def _dot_nt(a, b):
    return lax.dot_general(a, b, (((1,), (1,)), ((), ())), preferred_element_type=F32)


def _gelu(x):
    return 0.5 * x * (1.0 + lax.erf(x * (2.0 ** -0.5)))


def _params(*sem):
    return pltpu.CompilerParams(dimension_semantics=sem, vmem_limit_bytes=VMEM_LIMIT)


def _rope_chunk(a, cos, sin, first_half):
    partner = jnp.where(first_half, pltpu.roll(a, LANES - 32, 1), pltpu.roll(a, 32, 1))
    return a * cos + partner * sin


def _norm_matmul_kernel(*refs, rope, emit_xn, tm, tn):
    x_ref, g_ref, w_ref, *rest = refs
    if rope is not None:
        cos_ref, sin_ref, *rest = rest
    o_ref, *rest = rest
    if emit_xn:
        xn_out_ref, *rest = rest
    (xn_ref,) = rest
    j = pl.program_id(1)

    @pl.when(j == 0)
    def _():
        x = x_ref[...].astype(F32)
        ms = jnp.mean(x * x, axis=-1, keepdims=True)
        xn = (x * lax.rsqrt(ms + NORM_EPS) * g_ref[...]).astype(BF16)
        xn_ref[...] = xn
        if emit_xn:
            xn_out_ref[...] = xn

    acc = _dot(xn_ref[...], w_ref[...])
    if rope is None:
        o_ref[...] = acc.astype(o_ref.dtype)
        return
    tile_lo, tile_hi, chunk_mask = rope
    in_range = jnp.logical_and(j >= tile_lo, j < tile_hi)

    @pl.when(in_range)
    def _():
        cos = cos_ref[...]
        sin = sin_ref[...]
        lane = lax.broadcasted_iota(jnp.int32, (tm, LANES), 1)
        first_half = (lane & 63) < 32
        for c in range(tn // LANES):
            a = acc[:, c * LANES:(c + 1) * LANES]
            if chunk_mask[c]:
                a = _rope_chunk(a, cos, sin, first_half)
            o_ref[:, c * LANES:(c + 1) * LANES] = a.astype(o_ref.dtype)

    @pl.when(jnp.logical_not(in_range))
    def _():
        o_ref[...] = acc.astype(o_ref.dtype)


def norm_matmul(x, g, w, *, tm, tn, out_dtype, x_col_block=0, rope=None, cos=None, sin=None,
                emit_xn=False, name=None):
    T = x.shape[0]
    K, N = w.shape
    tm = min(tm, T)
    assert T % tm == 0 and N % tn == 0 and tn % LANES == 0
    in_specs = [pl.BlockSpec((tm, K), lambda i, j: (i, x_col_block)),
                pl.BlockSpec((1, K), lambda i, j: (0, 0)),
                pl.BlockSpec((K, tn), lambda i, j: (0, j))]
    args = [x, g.reshape(1, K).astype(F32), w]
    if rope is not None:
        in_specs += [pl.BlockSpec((tm, LANES), lambda i, j: (i, 0))] * 2
        args += [cos, sin]
    out_shape = [jax.ShapeDtypeStruct((T, N), out_dtype)]
    out_specs = [pl.BlockSpec((tm, tn), lambda i, j: (i, j))]
    if emit_xn:
        out_shape.append(jax.ShapeDtypeStruct((T, K), BF16))
        out_specs.append(pl.BlockSpec((tm, K), lambda i, j: (i, 0)))
    res = pl.pallas_call(
        functools.partial(_norm_matmul_kernel, rope=rope, emit_xn=emit_xn, tm=tm, tn=tn),
        out_shape=out_shape, grid=(T // tm, N // tn), in_specs=in_specs, out_specs=out_specs,
        scratch_shapes=[pltpu.VMEM((tm, K), BF16)],
        compiler_params=_params("parallel", "arbitrary"), name=name)(*args)
    return res if emit_xn else res[0]


def _matmul_residual_kernel(*refs, n_pairs):
    a_refs = refs[:n_pairs]
    w_refs = refs[n_pairs:2 * n_pairs]
    res_ref, o_ref = refs[2 * n_pairs:]
    acc = res_ref[...]
    for a_ref, w_ref in zip(a_refs, w_refs):
        acc = acc + _dot(a_ref[...], w_ref[...])
    o_ref[...] = acc


def matmul_residual(pairs, res, *, tm, tn, name=None):
    T, N = res.shape
    tm = min(tm, T)
    assert T % tm == 0 and N % tn == 0
    in_specs = [pl.BlockSpec((tm, a.shape[1]), lambda i, j: (i, 0)) for a, _ in pairs]
    in_specs += [pl.BlockSpec((w.shape[0], tn), lambda i, j: (0, j)) for _, w in pairs]
    in_specs += [pl.BlockSpec((tm, tn), lambda i, j: (i, j))]
    return pl.pallas_call(
        functools.partial(_matmul_residual_kernel, n_pairs=len(pairs)),
        out_shape=jax.ShapeDtypeStruct((T, N), F32), grid=(T // tm, N // tn),
        in_specs=in_specs, out_specs=pl.BlockSpec((tm, tn), lambda i, j: (i, j)),
        compiler_params=_params("parallel", "arbitrary"), name=name,
    )(*[a for a, _ in pairs], *[w for _, w in pairs], res)


def _flash_q_block(qq, load_k, load_v, qi, tq, dv):
    rows = qq.shape[0]

    def kv_step(j, carry, masked):
        m, l, acc = carry
        s = _dot_nt(qq, load_k(j))
        if masked:
            row = lax.broadcasted_iota(jnp.int32, (rows, tq), 0)
            col = lax.broadcasted_iota(jnp.int32, (rows, tq), 1)
            s = jnp.where(col <= (row & (tq - 1)), s, -jnp.inf)
        m_new = jnp.maximum(m, jnp.max(s, axis=1, keepdims=True))
        alpha = jnp.exp(m - m_new)
        p = jnp.exp(s - m_new)
        l = alpha * l + jnp.sum(p, axis=1, keepdims=True)
        acc = alpha * acc + _dot(p.astype(BF16), load_v(j))
        return m_new, l, acc

    init = (jnp.full((rows, 1), -jnp.inf, F32), jnp.zeros((rows, 1), F32), jnp.zeros((rows, dv), F32))
    carry = lax.fori_loop(0, qi, functools.partial(kv_step, masked=False), init)
    _, l, acc = kv_step(qi, carry, True)
    return acc / l


def _diff_attn_kernel(lq1_ref, lk1_ref, lq2_ref, lk2_ref, q_ref, k_ref, v_ref, g_ref, o_ref,
                      *, seq, tq, lam_init):
    lam = (jnp.exp(jnp.sum(lq1_ref[...] * lk1_ref[...], keepdims=True))
           - jnp.exp(jnp.sum(lq2_ref[...] * lk2_ref[...], keepdims=True)) + lam_init)
    lane = lax.broadcasted_iota(jnp.int32, (tq, LANES), 1)
    scale = DIFF_HEAD_DIM ** -0.5

    def load_k(j):
        return k_ref[pl.ds(pl.multiple_of(j * tq, tq), tq), :]

    def load_v(j):
        return v_ref[pl.ds(pl.multiple_of(j * tq, tq), tq), :]

    def q_block(qi, carry):
        r0 = pl.multiple_of(qi * tq, tq)
        qs = q_ref[pl.ds(r0, tq), :].astype(F32) * scale
        q1 = jnp.where(lane < DIFF_HEAD_DIM, qs, 0.0).astype(BF16)
        q2 = jnp.where(lane >= DIFF_HEAD_DIM, qs, 0.0).astype(BF16)
        o = _flash_q_block(jnp.concatenate([q1, q2], axis=0), load_k, load_v, qi, tq, LANES)
        o = o[:tq] - lam * o[tq:]
        ms = jnp.mean(o * o, axis=-1, keepdims=True)
        o = (o * lax.rsqrt(ms + NORM_EPS) * g_ref[...]) * (1.0 - lam_init)
        o_ref[pl.ds(r0, tq), :] = o.astype(o_ref.dtype)
        return carry

    lax.fori_loop(0, seq // tq, q_block, 0)


def diff_attention(proj, lq1, lk1, lq2, lk2, subln, *, batch, seq, lam_init, tq=256):
    T = proj.shape[0]
    H = DIFF_HEADS
    tq = min(tq, seq)
    vec = lambda a: a.reshape(1, -1).astype(F32)
    small = pl.BlockSpec((1, DIFF_HEAD_DIM), lambda b, h: (0, 0))
    return pl.pallas_call(
        functools.partial(_diff_attn_kernel, seq=seq, tq=tq, lam_init=lam_init),
        out_shape=jax.ShapeDtypeStruct((T, H * LANES), BF16), grid=(batch, H),
        in_specs=[small, small, small, small,
                  pl.BlockSpec((seq, LANES), lambda b, h: (b, h)),
                  pl.BlockSpec((seq, LANES), lambda b, h: (b, H + h)),
                  pl.BlockSpec((seq, LANES), lambda b, h: (b, 2 * H + h)),
                  pl.BlockSpec((1, LANES), lambda b, h: (0, 0))],
        out_specs=pl.BlockSpec((seq, LANES), lambda b, h: (b, h)),
        compiler_params=_params("parallel", "parallel"), name="diff_attention",
    )(vec(lq1), vec(lk1), vec(lq2), vec(lk2), proj, proj, proj, vec(subln))


def _mla_attn_kernel(q_ref, kn_ref, kr_ref, v_ref, o_ref, *, seq, tq, scale):
    def load_k(j):
        rows = pl.ds(pl.multiple_of(j * tq, tq), tq)
        return jnp.concatenate([kn_ref[rows, :], kr_ref[rows, :]], axis=1)

    def load_v(j):
        return v_ref[pl.ds(pl.multiple_of(j * tq, tq), tq), :]

    def q_block(qi, carry):
        r0 = pl.multiple_of(qi * tq, tq)
        qq = (q_ref[pl.ds(r0, tq), :].astype(F32) * scale).astype(BF16)
        o = _flash_q_block(qq, load_k, load_v, qi, tq, LANES)
        o_ref[pl.ds(r0, tq), :] = o.astype(o_ref.dtype)
        return carry

    lax.fori_loop(0, seq // tq, q_block, 0)


def mla_attention(q, kv, proj, *, batch, seq, tq=256):
    T = q.shape[0]
    tq = min(tq, seq)
    kr_block = (MLA_Q_RANK + MLA_KV_RANK) // LANES
    return pl.pallas_call(
        functools.partial(_mla_attn_kernel, seq=seq, tq=tq, scale=(MLA_NOPE + MLA_ROPE) ** -0.5),
        out_shape=jax.ShapeDtypeStruct((T, MLA_HEADS * LANES), BF16), grid=(batch, MLA_HEADS),
        in_specs=[pl.BlockSpec((seq, MLA_QK_PAD), lambda b, h: (b, h)),
                  pl.BlockSpec((seq, LANES), lambda b, h: (b, 2 * h)),
                  pl.BlockSpec((seq, LANES), lambda b, h: (b, kr_block)),
                  pl.BlockSpec((seq, LANES), lambda b, h: (b, 2 * h + 1))],
        out_specs=pl.BlockSpec((seq, LANES), lambda b, h: (b, h)),
        compiler_params=_params("parallel", "parallel"), name="mla_attention",
    )(q, kv, proj, kv)


def _xa_kernel(q_ref, k_ref, v_ref, o_ref, *, scale):
    q = (q_ref[...].astype(F32) * scale).astype(BF16)
    s = _dot_nt(q, k_ref[...])
    p = jnp.exp(s - jnp.max(s, axis=1, keepdims=True))
    l = jnp.sum(p, axis=1, keepdims=True)
    o_ref[...] = (_dot(p.astype(BF16), v_ref[...]) / l).astype(o_ref.dtype)


def cross_attention(q, kv, *, batch, seq, mem_len, tq=1024):
    T, D = q.shape
    hd = D // XA_HEADS
    tq = min(tq, seq)
    nq = seq // tq
    return pl.pallas_call(
        functools.partial(_xa_kernel, scale=hd ** -0.5),
        out_shape=jax.ShapeDtypeStruct((T, D), BF16), grid=(batch, XA_HEADS, nq),
        in_specs=[pl.BlockSpec((tq, hd), lambda b, h, i: (b * nq + i, h)),
                  pl.BlockSpec((mem_len, hd), lambda b, h, i: (b, h)),
                  pl.BlockSpec((mem_len, hd), lambda b, h, i: (b, XA_HEADS + h))],
        out_specs=pl.BlockSpec((tq, hd), lambda b, h, i: (b * nq + i, h)),
        compiler_params=_params("parallel", "parallel", "parallel"), name="cross_attention",
    )(q, kv, kv)


def _s5_kernel(u_ref, bmat_ref, are_ref, aim_ref, cmat_ref, d_ref, wglu_ref, bglu_ref, o_ref,
               bu_ref, h_ref, *, batch, tc, ns, lane_chunk):
    @pl.when(pl.program_id(0) == 0)
    def _():
        h_ref[...] = jnp.zeros_like(h_ref)

    u = u_ref[...]
    bu_ref[...] = _dot(u, bmat_ref[...])
    for lc in range(ns // lane_chunk):
        re = slice(lc * lane_chunk, (lc + 1) * lane_chunk)
        im = slice(ns + lc * lane_chunk, ns + (lc + 1) * lane_chunk)
        ar = jnp.broadcast_to(are_ref[:, re], (batch, lane_chunk))
        ai = jnp.broadcast_to(aim_ref[:, re], (batch, lane_chunk))

        def step(t, carry, re=re, im=im, ar=ar, ai=ai):
            hr, hi = carry
            rows = pl.ds(pl.multiple_of(t * batch, batch), batch)
            nr = ar * hr - ai * hi + bu_ref[rows, re]
            ni = ar * hi + ai * hr + bu_ref[rows, im]
            bu_ref[rows, re] = nr
            bu_ref[rows, im] = ni
            return nr, ni

        hr, hi = lax.fori_loop(0, tc, step, (h_ref[:, re], h_ref[:, im]), unroll=4)
        h_ref[:, re] = hr
        h_ref[:, im] = hi
    y = _dot(bu_ref[...].astype(BF16), cmat_ref[...]) + d_ref[...] * u.astype(F32)
    g = _gelu(y)
    z = _dot(g.astype(BF16), wglu_ref[...]) + bglu_ref[...]
    o_ref[...] = (g / (1.0 + jnp.exp(-z))).astype(o_ref.dtype)


def s5_mixer(u_tm, bmat, a_re, a_im, cmat, d, w_glu, b_glu, *, batch, seq, tc=64):
    W = u_tm.shape[1]
    ns = a_re.shape[1]
    tc = min(tc, seq)
    rows = tc * batch
    full = lambda a: pl.BlockSpec(a.shape, lambda t: (0,) * a.ndim)
    args = (bmat, a_re, a_im, cmat, d, w_glu, b_glu)
    return pl.pallas_call(
        functools.partial(_s5_kernel, batch=batch, tc=tc, ns=ns, lane_chunk=min(512, ns)),
        out_shape=jax.ShapeDtypeStruct(u_tm.shape, BF16), grid=(seq // tc,),
        in_specs=[pl.BlockSpec((rows, W), lambda t: (t, 0))] + [full(a) for a in args],
        out_specs=pl.BlockSpec((rows, W), lambda t: (t, 0)),
        scratch_shapes=[pltpu.VMEM((rows, 2 * ns), F32), pltpu.VMEM((batch, 2 * ns), F32)],
        compiler_params=_params("arbitrary"), name="s5_mixer",
    )(u_tm, *args)


def _top_sorted(s, k):
    rows = []
    for _ in range(k):
        mx = jnp.max(s, axis=0, keepdims=True)
        rows.append(mx)
        s = jnp.where(s >= mx, -jnp.inf, s)
    return jnp.concatenate(rows, axis=0)


def _peer_score_kernel(q_ref, keys_ref, s1_ref, s2_ref, a_ref, b_ref, tau_ref, *, tm):
    s1 = _dot_nt(keys_ref[0], q_ref[:, :PEER_HALF])
    s2 = _dot_nt(keys_ref[1], q_ref[:, PEER_HALF:])
    s1_ref[...] = s1
    s2_ref[...] = s2
    K = PEER_TOPK
    for c in range(tm // LANES):
        lanes = slice(c * LANES, (c + 1) * LANES)
        x1 = s1[:, lanes]
        x2 = s2[:, lanes]
        t1 = _top_sorted(x1, K)
        t2 = _top_sorted(x2, K)
        cands = [t1[0:1] + t2]
        cands += [t1[i:i + 1] + t2[0:SUBLANES] for i in range(1, SUBLANES)]
        cands += [t1[SUBLANES:K] + t2[0:1]]
        cand = jnp.concatenate(cands, axis=0)
        best = _top_sorted(cand, K)
        tau = best[K - 1:K]
        m = best[0:1]
        z = jnp.sum(jnp.where(cand >= tau, jnp.exp(cand - m), 0.0), axis=0, keepdims=True)
        tau_ref[:, lanes] = tau
        a_ref[:, lanes] = jnp.exp(x1 - t1[0:1]) / z
        b_ref[:, lanes] = jnp.exp(x2 - t2[0:1])


def peer_scores(q, keys, *, tm=512):
    T = q.shape[0]
    tm = min(tm, T)
    H = PEER_HEADS
    big = jax.ShapeDtypeStruct((H, PEER_N_KEYS, T), F32)
    big_spec = pl.BlockSpec((None, PEER_N_KEYS, tm), lambda i, h: (h, 0, i))
    return pl.pallas_call(
        functools.partial(_peer_score_kernel, tm=tm),
        out_shape=[big, big, big, big, jax.ShapeDtypeStruct((H, 1, T), F32)],
        grid=(T // tm, H),
        in_specs=[pl.BlockSpec((tm, 2 * PEER_HALF), lambda i, h: (i, h)),
                  pl.BlockSpec((2, PEER_N_KEYS, PEER_HALF), lambda i, h: (h, 0, 0))],
        out_specs=[big_spec, big_spec, big_spec, big_spec,
                   pl.BlockSpec((None, 1, tm), lambda i, h: (h, 0, i))],
        compiler_params=_params("parallel", "parallel"), name="peer_scores",
    )(q, keys)


def _peer_mix_kernel(h_ref, u_ref, vt_ref, s1_ref, s2_ref, a_ref, b_ref, tau_ref, x_ref, g_ref,
                     o_ref, acc_ref, st_ref, wt_ref, *, tm, te, final_norm):
    j = pl.program_id(1)
    n_i = te // PEER_N_KEYS
    jblocks = PEER_N_KEYS // SUBLANES

    @pl.when(j == 0)
    def _():
        acc_ref[...] = jnp.zeros_like(acc_ref)

    st_ref[...] = _dot_nt(u_ref[...], h_ref[...])

    def body(idx, carry):
        c = idx // jblocks
        jb = idx - c * jblocks
        lanes = pl.ds(pl.multiple_of(c * LANES, LANES), LANES)
        rows = pl.ds(pl.multiple_of(jb * SUBLANES, SUBLANES), SUBLANES)
        s2 = [s2_ref[h, rows, lanes] for h in range(PEER_HEADS)]
        b = [b_ref[h, rows, lanes] for h in range(PEER_HEADS)]
        tau = [tau_ref[h, :, lanes] for h in range(PEER_HEADS)]
        for ii in range(n_i):
            g = jnp.zeros((SUBLANES, LANES), F32)
            for h in range(PEER_HEADS):
                s1 = s1_ref[h, ii:ii + 1, lanes]
                a = a_ref[h, ii:ii + 1, lanes]
                g = g + jnp.where(s1 + s2[h] >= tau[h], a * b[h], 0.0)
            srow = pl.ds(pl.multiple_of(ii * PEER_N_KEYS + jb * SUBLANES, SUBLANES), SUBLANES)
            wt_ref[srow, lanes] = g * _gelu(st_ref[srow, lanes])
        return carry

    lax.fori_loop(0, (tm // LANES) * jblocks, body, 0)
    acc_ref[...] += _dot(vt_ref[...], wt_ref[...].astype(BF16))

    @pl.when(j == pl.num_programs(1) - 1)
    def _():
        y = x_ref[...] + acc_ref[...].T
        if final_norm:
            ms = jnp.mean(y * y, axis=-1, keepdims=True)
            y = y * lax.rsqrt(ms + NORM_EPS) * g_ref[...]
        o_ref[...] = y


def peer_mix(h, u, vt, s1, s2, a, b, tau, x, g_final, *, final_norm, tm=512, te=512):
    T, D = x.shape
    E = u.shape[0]
    tm = min(tm, T)
    H = PEER_HEADS
    tok = pl.BlockSpec((tm, D), lambda i, j: (i, 0))
    n_i = te // PEER_N_KEYS
    sc = pl.BlockSpec((H, PEER_N_KEYS, tm), lambda i, j: (0, 0, i))
    grp = pl.BlockSpec((H, None, n_i, tm), lambda i, j: (0, j, 0, i))
    s1 = s1.reshape(H, PEER_N_KEYS // n_i, n_i, T)
    a = a.reshape(H, PEER_N_KEYS // n_i, n_i, T)
    return pl.pallas_call(
        functools.partial(_peer_mix_kernel, tm=tm, te=te, final_norm=final_norm),
        out_shape=jax.ShapeDtypeStruct((T, D), F32), grid=(T // tm, E // te),
        in_specs=[tok,
                  pl.BlockSpec((te, D), lambda i, j: (j, 0)),
                  pl.BlockSpec((D, te), lambda i, j: (0, j)),
                  grp, sc, grp, sc,
                  pl.BlockSpec((H, 1, tm), lambda i, j: (0, 0, i)),
                  tok,
                  pl.BlockSpec((1, D), lambda i, j: (0, 0))],
        out_specs=tok,
        scratch_shapes=[pltpu.VMEM((D, tm), F32), pltpu.VMEM((te, tm), F32), pltpu.VMEM((te, tm), F32)],
        compiler_params=_params("parallel", "arbitrary"), name="peer_mix",
    )(h, u, vt, s1, s2, a, b, tau, x, g_final.reshape(1, D).astype(F32))


def _rope_tables(positions):
    half = DIFF_HEAD_DIM // 2
    inv_freq = ROPE_THETA ** (-jnp.arange(half, dtype=F32) * 2.0 / DIFF_HEAD_DIM)
    ang = positions.astype(F32).reshape(-1, 1) * inv_freq
    cos, sin = jnp.cos(ang), jnp.sin(ang)
    return jnp.tile(cos, (1, 4)), jnp.concatenate([-sin, sin, -sin, sin], axis=1)


def _s5_params(a_re, a_im, log_step, b_re, b_im, c_re, c_im):
    G, N = a_re.shape
    P = b_re.shape[-1]
    step = jnp.exp(log_step.astype(F32))[:, None]
    mag = jnp.exp(a_re * step)
    ab_re = mag * jnp.cos(a_im * step)
    ab_im = mag * jnp.sin(a_im * step)
    den = a_re * a_re + a_im * a_im
    num_re = ab_re - 1.0
    f_re = (num_re * a_re + ab_im * a_im) / den
    f_im = (ab_im * a_re - num_re * a_im) / den
    bb_re = f_re[..., None] * b_re - f_im[..., None] * b_im
    bb_im = f_re[..., None] * b_im + f_im[..., None] * b_re
    eye = jnp.eye(G, dtype=F32)
    blk_in = lambda m: jnp.einsum('gnp,gh->gphn', m, eye).reshape(G * P, G * N)
    blk_out = lambda m: jnp.einsum('gpn,gh->gnhp', m, eye).reshape(G * N, G * P)
    bmat = jnp.concatenate([blk_in(bb_re), blk_in(bb_im)], axis=1).astype(BF16)
    cmat = jnp.concatenate([blk_out(c_re), -blk_out(c_im)], axis=0).astype(BF16)
    return bmat, ab_re.reshape(1, G * N), ab_im.reshape(1, G * N), cmat


def ab_block(x, i, layer, cos, sin, ab_norm, ab_w_in, ab_w_out, diff_lq1, diff_lk1, diff_lq2, diff_lk2,
             diff_subln, s5_a_re, s5_a_im, s5_log_step, s5_b_re, s5_b_im, s5_c_re, s5_c_im, s5_d,
             s5_w_glu, s5_b_glu, *, batch, seq):
    T = x.shape[0]
    diff_w = DIFF_HEADS * 2 * DIFF_HEAD_DIM
    s5_w = ab_w_in.shape[2] - 3 * diff_w
    lam_init = 0.8 - 0.6 * math.exp(-0.3 * layer)
    proj = norm_matmul(x, ab_norm[i], ab_w_in[i].astype(BF16), tm=1024, tn=512, out_dtype=BF16,
                       rope=(0, 2 * diff_w // 512, (True,) * 4), cos=cos, sin=sin, name="ab_in")
    a_out = diff_attention(proj, diff_lq1[i], diff_lk1[i], diff_lq2[i], diff_lk2[i], diff_subln[i],
                           batch=batch, seq=seq, lam_init=lam_init)
    bmat, a_re, a_im, cmat = _s5_params(s5_a_re[i], s5_a_im[i], s5_log_step[i], s5_b_re[i], s5_b_im[i],
                                        s5_c_re[i], s5_c_im[i])
    u_tm = proj[:, 3 * diff_w:].reshape(batch, seq, s5_w).transpose(1, 0, 2).reshape(T, s5_w)
    b_tm = s5_mixer(u_tm, bmat, a_re, a_im, cmat, s5_d[i].reshape(1, s5_w).astype(F32),
                    s5_w_glu[i].astype(BF16), s5_b_glu[i].reshape(1, s5_w).astype(F32),
                    batch=batch, seq=seq)
    b_out = b_tm.reshape(seq, batch, s5_w).transpose(1, 0, 2).reshape(T, s5_w)
    w_out = ab_w_out[i].astype(BF16)
    return matmul_residual([(a_out, w_out[:diff_w]), (b_out, w_out[diff_w:])], x, tm=1024, tn=512,
                           name="ab_out")


def mla_block(x, i, cos, sin, mla_norm, mla_w_in, mla_q_norm, mla_kv_norm, mla_w_uq, mla_w_ukv, mla_w_o,
              *, batch, seq):
    in_w = mla_w_in.shape[2]
    in_pad = (MLA_Q_RANK + MLA_KV_RANK) + LANES
    w_in = jnp.pad(mla_w_in[i], ((0, 0), (0, in_pad - in_w))).astype(BF16)
    proj = norm_matmul(x, mla_norm[i], w_in, tm=1024, tn=in_pad // 3, out_dtype=BF16,
                       rope=(2, 3, (False, False, True)), cos=cos, sin=sin, name="mla_in")
    w_uq = mla_w_uq[i].reshape(MLA_Q_RANK, MLA_HEADS, MLA_NOPE + MLA_ROPE)
    w_uq = jnp.pad(w_uq, ((0, 0), (0, 0), (0, MLA_QK_PAD - MLA_NOPE - MLA_ROPE)))
    w_uq = w_uq.reshape(MLA_Q_RANK, MLA_HEADS * MLA_QK_PAD).astype(BF16)
    q = norm_matmul(proj, mla_q_norm[i], w_uq, tm=1024, tn=512, out_dtype=BF16, x_col_block=0,
                    rope=(0, MLA_HEADS * MLA_QK_PAD // 512, (False, True, False, True)),
                    cos=cos, sin=sin, name="mla_q")
    kv = norm_matmul(proj, mla_kv_norm[i], mla_w_ukv[i].astype(BF16), tm=1024, tn=512, out_dtype=BF16,
                     x_col_block=MLA_Q_RANK // MLA_KV_RANK, name="mla_kv")
    o = mla_attention(q, kv, proj, batch=batch, seq=seq)
    return matmul_residual([(o, mla_w_o[i].astype(BF16))], x, tm=1024, tn=512, name="mla_out")


def xa_block(x, mem, layer, xa_norm, xa_mem_norm, xa_w_q, xa_w_kv, xa_w_o, *, batch, seq):
    mem_len = mem.shape[0] // batch
    q = norm_matmul(x, xa_norm[layer], xa_w_q[layer].astype(BF16), tm=1024, tn=512, out_dtype=BF16,
                    name="xa_q")
    kv = norm_matmul(mem, xa_mem_norm[layer], xa_w_kv[layer].astype(BF16), tm=1024, tn=512,
                     out_dtype=BF16, name="xa_kv")
    o = cross_attention(q, kv, batch=batch, seq=seq, mem_len=mem_len)
    return matmul_residual([(o, xa_w_o[layer].astype(BF16))], x, tm=1024, tn=512, name="xa_out")


def peer_block(x, layer, ffn_norm, peer_w_query, peer_sub_keys, peer_u, peer_v, final_norm, *, last):
    pq, hn = norm_matmul(x, ffn_norm[layer], peer_w_query[layer].astype(BF16), tm=1024, tn=512,
                         out_dtype=BF16, emit_xn=True, name="peer_q")
    keys = peer_sub_keys[layer].reshape(2 * PEER_HEADS, PEER_N_KEYS, PEER_HALF).astype(BF16)
    s1, s2, a, b, tau = peer_scores(pq, keys)
    return peer_mix(hn, peer_u[layer].astype(BF16), peer_v[layer].astype(BF16).T, s1, s2, a, b, tau,
                    x, final_norm, final_norm=last)


def kernel(x, mem, positions, ab_norm, ab_w_in, ab_w_out, diff_lq1, diff_lk1, diff_lq2, diff_lk2, diff_subln, s5_a_re, s5_a_im, s5_log_step, s5_b_re, s5_b_im, s5_c_re, s5_c_im, s5_d, s5_w_glu, s5_b_glu, mla_norm, mla_w_in, mla_q_norm, mla_kv_norm, mla_w_uq, mla_w_ukv, mla_w_o, xa_norm, xa_mem_norm, xa_w_q, xa_w_kv, xa_w_o, ffn_norm, peer_w_query, peer_sub_keys, peer_u, peer_v, final_norm):
    batch, seq, D = x.shape
    depth = xa_norm.shape[0]
    x = x.reshape(batch * seq, D)
    mem = mem.reshape(-1, D)
    cos, sin = _rope_tables(positions)
    for layer in range(depth):
        i = layer // 2
        if layer % 2 == 0:
            x = ab_block(x, i, layer, cos, sin, ab_norm, ab_w_in, ab_w_out, diff_lq1, diff_lk1, diff_lq2,
                         diff_lk2, diff_subln, s5_a_re, s5_a_im, s5_log_step, s5_b_re, s5_b_im, s5_c_re,
                         s5_c_im, s5_d, s5_w_glu, s5_b_glu, batch=batch, seq=seq)
        else:
            x = mla_block(x, i, cos, sin, mla_norm, mla_w_in, mla_q_norm, mla_kv_norm, mla_w_uq, mla_w_ukv,
                          mla_w_o, batch=batch, seq=seq)
        x = xa_block(x, mem, layer, xa_norm, xa_mem_norm, xa_w_q, xa_w_kv, xa_w_o, batch=batch, seq=seq)
        x = peer_block(x, layer, ffn_norm, peer_w_query, peer_sub_keys, peer_u, peer_v, final_norm,
                       last=(layer == depth - 1))
    return x.reshape(batch, seq, D)
```

```python
import functools
import math

import jax
import jax.numpy as jnp
from jax import lax
from jax.experimental import pallas as pl
from jax.experimental.pallas import tpu as pltpu

F32 = jnp.float32
BF16 = jnp.bfloat16

NORM_EPS = 1e-6
ROPE_THETA = 10000.0
LANES = 128
SUBLANES = 8
MXU_DEPTH = 256
VMEM_LIMIT = 56 * 1024 * 1024

DIFF_HEADS = 12
DIFF_HEAD_DIM = 64
S5_GROUP = 16
S5_STATE = 64
MLA_HEADS = 16
MLA_Q_RANK = 768
MLA_KV_RANK = 256
MLA_NOPE = 128
MLA_ROPE = 64
MLA_QK_PAD = 256
XA_HEADS = 4
PEER_HEADS = 8
PEER_N_KEYS = 128
PEER_TOPK = 16
PEER_HALF = 128


def _dot(a, b):
    return jnp.dot(a, b, preferred_element_type=F32)


def _dot_nt(a, b):
    return lax.dot_general(a, b, (((1,), (1,)), ((), ())), preferred_element_type=F32)


def _gelu(x):
    return 0.5 * x * (1.0 + lax.erf(x * (2.0 ** -0.5)))


def _params(*sem):
    return pltpu.CompilerParams(dimension_semantics=sem, vmem_limit_bytes=VMEM_LIMIT)


def _rope_chunk(a, cos, sin, first_half):
    partner = jnp.where(first_half, pltpu.roll(a, LANES - 32, 1), pltpu.roll(a, 32, 1))
    return a * cos + partner * sin


def _norm_matmul_kernel(*refs, rope, emit_xn, tm, tn):
    x_ref, g_ref, w_ref, *rest = refs
    if rope is not None:
        cos_ref, sin_ref, *rest = rest
    o_ref, *rest = rest
    if emit_xn:
        xn_out_ref, *rest = rest
    (xn_ref,) = rest
    j = pl.program_id(1)

    @pl.when(j == 0)
    def _():
        x = x_ref[...].astype(F32)
        ms = jnp.mean(x * x, axis=-1, keepdims=True)
        xn = (x * lax.rsqrt(ms + NORM_EPS) * g_ref[...]).astype(BF16)
        xn_ref[...] = xn
        if emit_xn:
            xn_out_ref[...] = xn

    acc = _dot(xn_ref[...], w_ref[...])
    if rope is None:
        o_ref[...] = acc.astype(o_ref.dtype)
        return
    tile_lo, tile_hi, chunk_mask = rope
    in_range = jnp.logical_and(j >= tile_lo, j < tile_hi)

    @pl.when(in_range)
    def _():
        cos = cos_ref[...]
        sin = sin_ref[...]
        lane = lax.broadcasted_iota(jnp.int32, (tm, LANES), 1)
        first_half = (lane & 63) < 32
        for c in range(tn // LANES):
            a = acc[:, c * LANES:(c + 1) * LANES]
            if chunk_mask[c]:
                a = _rope_chunk(a, cos, sin, first_half)
            o_ref[:, c * LANES:(c + 1) * LANES] = a.astype(o_ref.dtype)

    @pl.when(jnp.logical_not(in_range))
    def _():
        o_ref[...] = acc.astype(o_ref.dtype)


def norm_matmul(x, g, w, *, tm, tn, out_dtype, x_col_block=0, rope=None, cos=None, sin=None,
                emit_xn=False, name=None):
    T = x.shape[0]
    K, N = w.shape
    tm = min(tm, T)
    assert T % tm == 0 and N % tn == 0 and tn % LANES == 0
    in_specs = [pl.BlockSpec((tm, K), lambda i, j: (i, x_col_block)),
                pl.BlockSpec((1, K), lambda i, j: (0, 0)),
                pl.BlockSpec((K, tn), lambda i, j: (0, j))]
    args = [x, g.reshape(1, K).astype(F32), w]
    if rope is not None:
        in_specs += [pl.BlockSpec((tm, LANES), lambda i, j: (i, 0))] * 2
        args += [cos, sin]
    out_shape = [jax.ShapeDtypeStruct((T, N), out_dtype)]
    out_specs = [pl.BlockSpec((tm, tn), lambda i, j: (i, j))]
    if emit_xn:
        out_shape.append(jax.ShapeDtypeStruct((T, K), BF16))
        out_specs.append(pl.BlockSpec((tm, K), lambda i, j: (i, 0)))
    res = pl.pallas_call(
        functools.partial(_norm_matmul_kernel, rope=rope, emit_xn=emit_xn, tm=tm, tn=tn),
        out_shape=out_shape, grid=(T // tm, N // tn), in_specs=in_specs, out_specs=out_specs,
        scratch_shapes=[pltpu.VMEM((tm, K), BF16)],
        compiler_params=_params("parallel", "arbitrary"), name=name)(*args)
    return res if emit_xn else res[0]


def _matmul_residual_kernel(*refs, n_pairs):
    a_refs = refs[:n_pairs]
    w_refs = refs[n_pairs:2 * n_pairs]
    res_ref, o_ref = refs[2 * n_pairs:]
    acc = res_ref[...]
    for a_ref, w_ref in zip(a_refs, w_refs):
        acc = acc + _dot(a_ref[...], w_ref[...])
    o_ref[...] = acc


def matmul_residual(pairs, res, *, tm, tn, name=None):
    T, N = res.shape
    tm = min(tm, T)
    assert T % tm == 0 and N % tn == 0
    in_specs = [pl.BlockSpec((tm, a.shape[1]), lambda i, j: (i, 0)) for a, _ in pairs]
    in_specs += [pl.BlockSpec((w.shape[0], tn), lambda i, j: (0, j)) for _, w in pairs]
    in_specs += [pl.BlockSpec((tm, tn), lambda i, j: (i, j))]
    return pl.pallas_call(
        functools.partial(_matmul_residual_kernel, n_pairs=len(pairs)),
        out_shape=jax.ShapeDtypeStruct((T, N), F32), grid=(T // tm, N // tn),
        in_specs=in_specs, out_specs=pl.BlockSpec((tm, tn), lambda i, j: (i, j)),
        compiler_params=_params("parallel", "arbitrary"), name=name,
    )(*[a for a, _ in pairs], *[w for _, w in pairs], res)


def _flash_q_block(qqs, load_k, load_v, qi, tq, dv):
    n = len(qqs)
    rows = qqs[0].shape[0]

    def kv_step(j, carry, masked):
        out = []
        for hh in range(n):
            m, l, acc = carry[hh]
            s = _dot_nt(qqs[hh], load_k(hh, j))
            if masked:
                row = lax.broadcasted_iota(jnp.int32, (rows, tq), 0)
                col = lax.broadcasted_iota(jnp.int32, (rows, tq), 1)
                s = jnp.where(col <= (row & (tq - 1)), s, -jnp.inf)
            m_new = jnp.maximum(m, jnp.max(s, axis=1, keepdims=True))
            alpha = jnp.exp(m - m_new)
            p = jnp.exp(s - m_new)
            l = alpha * l + jnp.sum(p, axis=1, keepdims=True)
            acc = alpha * acc + _dot(p.astype(BF16), load_v(hh, j))
            out.append((m_new, l, acc))
        return tuple(out)

    init = tuple((jnp.full((rows, 1), -jnp.inf, F32), jnp.zeros((rows, 1), F32),
                  jnp.zeros((rows, dv), F32)) for _ in range(n))
    carry = lax.fori_loop(0, qi, functools.partial(kv_step, masked=False), init)
    carry = kv_step(qi, carry, True)
    return [acc / l for _, l, acc in carry]


def _diff_attn_kernel(lq1_ref, lk1_ref, lq2_ref, lk2_ref, q_ref, k_ref, v_ref, g_ref, o_ref,
                      *, seq, tq, hp, lam_init):
    lam = (jnp.exp(jnp.sum(lq1_ref[...] * lk1_ref[...], keepdims=True))
           - jnp.exp(jnp.sum(lq2_ref[...] * lk2_ref[...], keepdims=True)) + lam_init)
    lane = lax.broadcasted_iota(jnp.int32, (tq, LANES), 1)
    scale = DIFF_HEAD_DIM ** -0.5
    head_lanes = lambda hh: slice(hh * LANES, (hh + 1) * LANES)

    def load_k(hh, j):
        return k_ref[pl.ds(pl.multiple_of(j * tq, tq), tq), head_lanes(hh)]

    def load_v(hh, j):
        return v_ref[pl.ds(pl.multiple_of(j * tq, tq), tq), head_lanes(hh)]

    def q_block(qi, carry):
        r0 = pl.multiple_of(qi * tq, tq)
        qqs = []
        for hh in range(hp):
            qs = q_ref[pl.ds(r0, tq), head_lanes(hh)].astype(F32) * scale
            q1 = jnp.where(lane < DIFF_HEAD_DIM, qs, 0.0).astype(BF16)
            q2 = jnp.where(lane >= DIFF_HEAD_DIM, qs, 0.0).astype(BF16)
            qqs.append(jnp.concatenate([q1, q2], axis=0))
        outs = _flash_q_block(qqs, load_k, load_v, qi, tq, LANES)
        for hh, o in enumerate(outs):
            o = o[:tq] - lam * o[tq:]
            ms = jnp.mean(o * o, axis=-1, keepdims=True)
            o = (o * lax.rsqrt(ms + NORM_EPS) * g_ref[...]) * (1.0 - lam_init)
            o_ref[pl.ds(r0, tq), head_lanes(hh)] = o.astype(o_ref.dtype)
        return carry

    lax.fori_loop(0, seq // tq, q_block, 0)


def diff_attention(proj, lq1, lk1, lq2, lk2, subln, *, batch, seq, lam_init, tq=256, hp=4):
    T = proj.shape[0]
    G = DIFF_HEADS // hp
    W = hp * LANES
    tq = min(tq, seq)
    vec = lambda a: a.reshape(1, -1).astype(F32)
    small = pl.BlockSpec((1, DIFF_HEAD_DIM), lambda b, g: (0, 0))
    return pl.pallas_call(
        functools.partial(_diff_attn_kernel, seq=seq, tq=tq, hp=hp, lam_init=lam_init),
        out_shape=jax.ShapeDtypeStruct((T, DIFF_HEADS * LANES), BF16), grid=(batch, G),
        in_specs=[small, small, small, small,
                  pl.BlockSpec((seq, W), lambda b, g: (b, g)),
                  pl.BlockSpec((seq, W), lambda b, g: (b, G + g)),
                  pl.BlockSpec((seq, W), lambda b, g: (b, 2 * G + g)),
                  pl.BlockSpec((1, LANES), lambda b, g: (0, 0))],
        out_specs=pl.BlockSpec((seq, W), lambda b, g: (b, g)),
        compiler_params=_params("parallel", "parallel"), name="diff_attention",
    )(vec(lq1), vec(lk1), vec(lq2), vec(lk2), proj, proj, proj, vec(subln))


def _mla_attn_kernel(q_ref, kv_ref, kr_ref, o_ref, *, seq, tq, hp, scale):
    def load_k(hh, j):
        rows = pl.ds(pl.multiple_of(j * tq, tq), tq)
        kn = kv_ref[rows, hh * MLA_QK_PAD:hh * MLA_QK_PAD + MLA_NOPE]
        return jnp.concatenate([kn, kr_ref[rows, :]], axis=1)

    def load_v(hh, j):
        rows = pl.ds(pl.multiple_of(j * tq, tq), tq)
        return kv_ref[rows, hh * MLA_QK_PAD + MLA_NOPE:(hh + 1) * MLA_QK_PAD]

    def q_block(qi, carry):
        r0 = pl.multiple_of(qi * tq, tq)
        qqs = [(q_ref[pl.ds(r0, tq), hh * MLA_QK_PAD:(hh + 1) * MLA_QK_PAD].astype(F32) * scale).astype(BF16)
               for hh in range(hp)]
        outs = _flash_q_block(qqs, load_k, load_v, qi, tq, LANES)
        for hh, o in enumerate(outs):
            o_ref[pl.ds(r0, tq), hh * LANES:(hh + 1) * LANES] = o.astype(o_ref.dtype)
        return carry

    lax.fori_loop(0, seq // tq, q_block, 0)


def mla_attention(q, kv, proj, *, batch, seq, tq=256, hp=8):
    T = q.shape[0]
    tq = min(tq, seq)
    kr_block = (MLA_Q_RANK + MLA_KV_RANK) // LANES
    return pl.pallas_call(
        functools.partial(_mla_attn_kernel, seq=seq, tq=tq, hp=hp, scale=(MLA_NOPE + MLA_ROPE) ** -0.5),
        out_shape=jax.ShapeDtypeStruct((T, MLA_HEADS * LANES), BF16), grid=(batch, MLA_HEADS // hp),
        in_specs=[pl.BlockSpec((seq, hp * MLA_QK_PAD), lambda b, g: (b, g)),
                  pl.BlockSpec((seq, hp * MLA_QK_PAD), lambda b, g: (b, g)),
                  pl.BlockSpec((seq, LANES), lambda b, g: (b, kr_block))],
        out_specs=pl.BlockSpec((seq, hp * LANES), lambda b, g: (b, g)),
        compiler_params=_params("parallel", "parallel"), name="mla_attention",
    )(q, kv, proj)


def _xa_kernel(q_ref, k_ref, v_ref, o_ref, *, scale):
    q = (q_ref[...].astype(F32) * scale).astype(BF16)
    s = _dot_nt(q, k_ref[...])
    p = jnp.exp(s - jnp.max(s, axis=1, keepdims=True))
    l = jnp.sum(p, axis=1, keepdims=True)
    o_ref[...] = (_dot(p.astype(BF16), v_ref[...]) / l).astype(o_ref.dtype)


def cross_attention(q, kv, *, batch, seq, mem_len, tq=1024):
    T, D = q.shape
    hd = D // XA_HEADS
    tq = min(tq, seq)
    nq = seq // tq
    return pl.pallas_call(
        functools.partial(_xa_kernel, scale=hd ** -0.5),
        out_shape=jax.ShapeDtypeStruct((T, D), BF16), grid=(batch, XA_HEADS, nq),
        in_specs=[pl.BlockSpec((tq, hd), lambda b, h, i: (b * nq + i, h)),
                  pl.BlockSpec((mem_len, hd), lambda b, h, i: (b, h)),
                  pl.BlockSpec((mem_len, hd), lambda b, h, i: (b, XA_HEADS + h))],
        out_specs=pl.BlockSpec((tq, hd), lambda b, h, i: (b * nq + i, h)),
        compiler_params=_params("parallel", "parallel", "parallel"), name="cross_attention",
    )(q, kv, kv)


def _s5_kernel(u_ref, bmat_ref, are_ref, aim_ref, cmat_ref, d_ref, wglu_ref, bglu_ref, o_ref,
               bu_ref, h_ref, *, batch, tc, ns, lane_chunk):
    @pl.when(pl.program_id(0) == 0)
    def _():
        h_ref[...] = jnp.zeros_like(h_ref)

    u = u_ref[...]
    bu_ref[...] = _dot(u, bmat_ref[...])
    for lc in range(ns // lane_chunk):
        re = slice(lc * lane_chunk, (lc + 1) * lane_chunk)
        im = slice(ns + lc * lane_chunk, ns + (lc + 1) * lane_chunk)
        ar = jnp.broadcast_to(are_ref[:, re], (batch, lane_chunk))
        ai = jnp.broadcast_to(aim_ref[:, re], (batch, lane_chunk))

        def step(t, carry, re=re, im=im, ar=ar, ai=ai):
            hr, hi = carry
            rows = pl.ds(pl.multiple_of(t * batch, batch), batch)
            nr = ar * hr - ai * hi + bu_ref[rows, re]
            ni = ar * hi + ai * hr + bu_ref[rows, im]
            bu_ref[rows, re] = nr
            bu_ref[rows, im] = ni
            return nr, ni

        hr, hi = lax.fori_loop(0, tc, step, (h_ref[:, re], h_ref[:, im]), unroll=4)
        h_ref[:, re] = hr
        h_ref[:, im] = hi
    y = _dot(bu_ref[...].astype(BF16), cmat_ref[...]) + d_ref[...] * u.astype(F32)
    g = _gelu(y)
    z = _dot(g.astype(BF16), wglu_ref[...]) + bglu_ref[...]
    o_ref[...] = (g / (1.0 + jnp.exp(-z))).astype(o_ref.dtype)


def s5_mixer(u_tm, bmat, a_re, a_im, cmat, d, w_glu, b_glu, *, batch, seq, tc=64):
    W = u_tm.shape[1]
    ns = a_re.shape[1]
    tc = min(tc, seq)
    rows = tc * batch
    full = lambda a: pl.BlockSpec(a.shape, lambda t: (0,) * a.ndim)
    args = (bmat, a_re, a_im, cmat, d, w_glu, b_glu)
    return pl.pallas_call(
        functools.partial(_s5_kernel, batch=batch, tc=tc, ns=ns, lane_chunk=min(512, ns)),
        out_shape=jax.ShapeDtypeStruct(u_tm.shape, BF16), grid=(seq // tc,),
        in_specs=[pl.BlockSpec((rows, W), lambda t: (t, 0))] + [full(a) for a in args],
        out_specs=pl.BlockSpec((rows, W), lambda t: (t, 0)),
        scratch_shapes=[pltpu.VMEM((rows, 2 * ns), F32), pltpu.VMEM((batch, 2 * ns), F32)],
        compiler_params=_params("arbitrary"), name="s5_mixer",
    )(u_tm, *args)


def _top_sorted(s, k):
    rows = []
    for _ in range(k):
        mx = jnp.max(s, axis=0, keepdims=True)
        rows.append(mx)
        s = jnp.where(s >= mx, -jnp.inf, s)
    return jnp.concatenate(rows, axis=0)


def _peer_score_kernel(q_ref, keys_ref, s1_ref, s2_ref, a_ref, b_ref, tau_ref, *, tm):
    s1 = _dot_nt(keys_ref[0], q_ref[:, :PEER_HALF])
    s2 = _dot_nt(keys_ref[1], q_ref[:, PEER_HALF:])
    s1_ref[...] = s1
    s2_ref[...] = s2
    K = PEER_TOPK
    for c in range(tm // LANES):
        lanes = slice(c * LANES, (c + 1) * LANES)
        x1 = s1[:, lanes]
        x2 = s2[:, lanes]
        t1 = _top_sorted(x1, K)
        t2 = _top_sorted(x2, K)
        cands = [t1[0:1] + t2]
        cands += [t1[i:i + 1] + t2[0:SUBLANES] for i in range(1, SUBLANES)]
        cands += [t1[SUBLANES:K] + t2[0:1]]
        cand = jnp.concatenate(cands, axis=0)
        best = _top_sorted(cand, K)
        tau = best[K - 1:K]
        m = best[0:1]
        z = jnp.sum(jnp.where(cand >= tau, jnp.exp(cand - m), 0.0), axis=0, keepdims=True)
        tau_ref[:, lanes] = tau
        a_ref[:, lanes] = jnp.exp(x1 - t1[0:1]) / z
        b_ref[:, lanes] = jnp.exp(x2 - t2[0:1])


def peer_scores(q, keys, *, tm=512):
    T = q.shape[0]
    tm = min(tm, T)
    H = PEER_HEADS
    big = jax.ShapeDtypeStruct((H, PEER_N_KEYS, T), F32)
    big_spec = pl.BlockSpec((None, PEER_N_KEYS, tm), lambda i, h: (h, 0, i))
    return pl.pallas_call(
        functools.partial(_peer_score_kernel, tm=tm),
        out_shape=[big, big, big, big, jax.ShapeDtypeStruct((H, 1, T), F32)],
        grid=(T // tm, H),
        in_specs=[pl.BlockSpec((tm, 2 * PEER_HALF), lambda i, h: (i, h)),
                  pl.BlockSpec((2, PEER_N_KEYS, PEER_HALF), lambda i, h: (h, 0, 0))],
        out_specs=[big_spec, big_spec, big_spec, big_spec,
                   pl.BlockSpec((None, 1, tm), lambda i, h: (h, 0, i))],
        compiler_params=_params("parallel", "parallel"), name="peer_scores",
    )(q, keys)


def _peer_mix_kernel(h_ref, u_ref, vt_ref, s1_ref, s2_ref, a_ref, b_ref, tau_ref, x_ref, g_ref,
                     o_ref, acc_ref, ht_ref, *, tm, te, final_norm):
    j = pl.program_id(1)
    n_i = te // PEER_N_KEYS
    heads = range(PEER_HEADS)

    @pl.when(j == 0)
    def _():
        acc_ref[...] = jnp.zeros_like(acc_ref)
        ht_ref[...] = h_ref[...].T

    n_c = tm // LANES
    n_jb = PEER_N_KEYS // SUBLANES
    g = [[[None] * n_c for _ in range(n_jb)] for _ in range(n_i)]
    u = u_ref[...]
    pack = 2 * SUBLANES
    n_r = te // pack
    k_blocks = u.shape[1] // MXU_DEPTH
    groups = [(c, jb) for c in range(n_c) for jb in range(n_jb)]
    per_chunk = len(groups) // n_r
    u_rows = []
    for r in range(n_r):
        tot = []
        for c, jb in groups[r * per_chunk:(r + 1) * per_chunk]:
            lanes = slice(c * LANES, (c + 1) * LANES)
            rows = slice(jb * SUBLANES, (jb + 1) * SUBLANES)
            tau = [tau_ref[h, :, lanes] for h in heads]
            s2 = [s2_ref[h, rows, lanes] for h in heads]
            b = [b_ref[h, rows, lanes] for h in heads]
            for ii in range(n_i):
                terms = [jnp.where(s1_ref[h, ii:ii + 1, lanes] + s2[h] >= tau[h],
                                   a_ref[h, ii:ii + 1, lanes] * b[h], 0.0) for h in heads]
                while len(terms) > 1:
                    terms = [x + y for x, y in zip(terms[0::2], terms[1::2])]
                g[ii][jb][c] = terms[0]
                tot.append(terms[0])
        while len(tot) > 2:
            tot = [x + y for x, y in zip(tot[0::2], tot[1::2])]
        zero = [pltpu.bitcast(lax.shift_right_logical(pltpu.bitcast(t, jnp.uint32), jnp.uint32(32)), F32)
                for t in tot]
        z = jnp.concatenate(zero, axis=0).astype(BF16)
        l0 = ((r * k_blocks) // n_r) * MXU_DEPTH
        row = u[r * pack:(r + 1) * pack, :]
        parts = [row[:, l0:l0 + LANES] + z, row[:, l0 + LANES:]]
        if l0:
            parts.insert(0, row[:, :l0])
        u_rows.append(jnp.concatenate(parts, axis=1))
    st = _dot(jnp.concatenate(u_rows, axis=0), ht_ref[...])
    gt = jnp.concatenate([jnp.concatenate(g[ii][jb], axis=1) for ii in range(n_i) for jb in range(n_jb)],
                         axis=0)
    acc_ref[...] += _dot(vt_ref[...], (gt * _gelu(st)).astype(BF16))

    @pl.when(j == pl.num_programs(1) - 1)
    def _():
        y = x_ref[...] + acc_ref[...].T
        if final_norm:
            ms = jnp.mean(y * y, axis=-1, keepdims=True)
            y = y * lax.rsqrt(ms + NORM_EPS) * g_ref[...]
        o_ref[...] = y


def peer_mix(h, u, vt, s1, s2, a, b, tau, x, g_final, *, final_norm, tm=512, te=512):
    T, D = x.shape
    E = u.shape[0]
    tm = min(tm, T)
    H = PEER_HEADS
    tok = pl.BlockSpec((tm, D), lambda i, j: (i, 0))
    n_i = te // PEER_N_KEYS
    sc = pl.BlockSpec((H, PEER_N_KEYS, tm), lambda i, j: (0, 0, i))
    grp = pl.BlockSpec((H, None, n_i, tm), lambda i, j: (0, j, 0, i))
    s1 = s1.reshape(H, PEER_N_KEYS // n_i, n_i, T)
    a = a.reshape(H, PEER_N_KEYS // n_i, n_i, T)
    return pl.pallas_call(
        functools.partial(_peer_mix_kernel, tm=tm, te=te, final_norm=final_norm),
        out_shape=jax.ShapeDtypeStruct((T, D), F32), grid=(T // tm, E // te),
        in_specs=[tok,
                  pl.BlockSpec((te, D), lambda i, j: (j, 0)),
                  pl.BlockSpec((D, te), lambda i, j: (0, j)),
                  grp, sc, grp, sc,
                  pl.BlockSpec((H, 1, tm), lambda i, j: (0, 0, i)),
                  tok,
                  pl.BlockSpec((1, D), lambda i, j: (0, 0))],
        out_specs=tok,
        scratch_shapes=[pltpu.VMEM((D, tm), F32), pltpu.VMEM((D, tm), BF16)],
        compiler_params=_params("parallel", "arbitrary"), name="peer_mix",
    )(h, u, vt, s1, s2, a, b, tau, x, g_final.reshape(1, D).astype(F32))


def _rope_tables(positions):
    half = DIFF_HEAD_DIM // 2
    inv_freq = ROPE_THETA ** (-jnp.arange(half, dtype=F32) * 2.0 / DIFF_HEAD_DIM)
    ang = positions.astype(F32).reshape(-1, 1) * inv_freq
    cos, sin = jnp.cos(ang), jnp.sin(ang)
    return jnp.tile(cos, (1, 4)), jnp.concatenate([-sin, sin, -sin, sin], axis=1)


def _s5_params(a_re, a_im, log_step, b_re, b_im, c_re, c_im):
    G, N = a_re.shape
    P = b_re.shape[-1]
    step = jnp.exp(log_step.astype(F32))[:, None]
    mag = jnp.exp(a_re * step)
    ab_re = mag * jnp.cos(a_im * step)
    ab_im = mag * jnp.sin(a_im * step)
    den = a_re * a_re + a_im * a_im
    num_re = ab_re - 1.0
    f_re = (num_re * a_re + ab_im * a_im) / den
    f_im = (ab_im * a_re - num_re * a_im) / den
    bb_re = f_re[..., None] * b_re - f_im[..., None] * b_im
    bb_im = f_re[..., None] * b_im + f_im[..., None] * b_re
    eye = jnp.eye(G, dtype=F32)
    blk_in = lambda m: jnp.einsum('gnp,gh->gphn', m, eye).reshape(G * P, G * N)
    blk_out = lambda m: jnp.einsum('gpn,gh->gnhp', m, eye).reshape(G * N, G * P)
    bmat = jnp.concatenate([blk_in(bb_re), blk_in(bb_im)], axis=1).astype(BF16)
    cmat = jnp.concatenate([blk_out(c_re), -blk_out(c_im)], axis=0).astype(BF16)
    return bmat, ab_re.reshape(1, G * N), ab_im.reshape(1, G * N), cmat


def ab_block(x, i, layer, cos, sin, ab_norm, ab_w_in, ab_w_out, diff_lq1, diff_lk1, diff_lq2, diff_lk2,
             diff_subln, s5_a_re, s5_a_im, s5_log_step, s5_b_re, s5_b_im, s5_c_re, s5_c_im, s5_d,
             s5_w_glu, s5_b_glu, *, batch, seq):
    T = x.shape[0]
    diff_w = DIFF_HEADS * 2 * DIFF_HEAD_DIM
    s5_w = ab_w_in.shape[2] - 3 * diff_w
    lam_init = 0.8 - 0.6 * math.exp(-0.3 * layer)
    proj = norm_matmul(x, ab_norm[i], ab_w_in[i].astype(BF16), tm=1024, tn=512, out_dtype=BF16,
                       rope=(0, 2 * diff_w // 512, (True,) * 4), cos=cos, sin=sin, name="ab_in")
    a_out = diff_attention(proj, diff_lq1[i], diff_lk1[i], diff_lq2[i], diff_lk2[i], diff_subln[i],
                           batch=batch, seq=seq, lam_init=lam_init)
    bmat, a_re, a_im, cmat = _s5_params(s5_a_re[i], s5_a_im[i], s5_log_step[i], s5_b_re[i], s5_b_im[i],
                                        s5_c_re[i], s5_c_im[i])
    u_tm = proj[:, 3 * diff_w:].reshape(batch, seq, s5_w).transpose(1, 0, 2).reshape(T, s5_w)
    b_tm = s5_mixer(u_tm, bmat, a_re, a_im, cmat, s5_d[i].reshape(1, s5_w).astype(F32),
                    s5_w_glu[i].astype(BF16), s5_b_glu[i].reshape(1, s5_w).astype(F32),
                    batch=batch, seq=seq)
    b_out = b_tm.reshape(seq, batch, s5_w).transpose(1, 0, 2).reshape(T, s5_w)
    w_out = ab_w_out[i].astype(BF16)
    return matmul_residual([(a_out, w_out[:diff_w]), (b_out, w_out[diff_w:])], x, tm=1024, tn=512,
                           name="ab_out")


def mla_block(x, i, cos, sin, mla_norm, mla_w_in, mla_q_norm, mla_kv_norm, mla_w_uq, mla_w_ukv, mla_w_o,
              *, batch, seq):
    in_w = mla_w_in.shape[2]
    in_pad = (MLA_Q_RANK + MLA_KV_RANK) + LANES
    w_in = jnp.pad(mla_w_in[i], ((0, 0), (0, in_pad - in_w))).astype(BF16)
    proj = norm_matmul(x, mla_norm[i], w_in, tm=1024, tn=in_pad // 3, out_dtype=BF16,
                       rope=(2, 3, (False, False, True)), cos=cos, sin=sin, name="mla_in")
    w_uq = mla_w_uq[i].reshape(MLA_Q_RANK, MLA_HEADS, MLA_NOPE + MLA_ROPE)
    w_uq = jnp.pad(w_uq, ((0, 0), (0, 0), (0, MLA_QK_PAD - MLA_NOPE - MLA_ROPE)))
    w_uq = w_uq.reshape(MLA_Q_RANK, MLA_HEADS * MLA_QK_PAD).astype(BF16)
    q = norm_matmul(proj, mla_q_norm[i], w_uq, tm=1024, tn=512, out_dtype=BF16, x_col_block=0,
                    rope=(0, MLA_HEADS * MLA_QK_PAD // 512, (False, True, False, True)),
                    cos=cos, sin=sin, name="mla_q")
    kv = norm_matmul(proj, mla_kv_norm[i], mla_w_ukv[i].astype(BF16), tm=1024, tn=512, out_dtype=BF16,
                     x_col_block=MLA_Q_RANK // MLA_KV_RANK, name="mla_kv")
    o = mla_attention(q, kv, proj, batch=batch, seq=seq)
    return matmul_residual([(o, mla_w_o[i].astype(BF16))], x, tm=1024, tn=512, name="mla_out")


def xa_block(x, mem, layer, xa_norm, xa_mem_norm, xa_w_q, xa_w_kv, xa_w_o, *, batch, seq):
    mem_len = mem.shape[0] // batch
    q = norm_matmul(x, xa_norm[layer], xa_w_q[layer].astype(BF16), tm=1024, tn=512, out_dtype=BF16,
                    name="xa_q")
    kv = norm_matmul(mem, xa_mem_norm[layer], xa_w_kv[layer].astype(BF16), tm=1024, tn=512,
                     out_dtype=BF16, name="xa_kv")
    o = cross_attention(q, kv, batch=batch, seq=seq, mem_len=mem_len)
    return matmul_residual([(o, xa_w_o[layer].astype(BF16))], x, tm=1024, tn=512, name="xa_out")


def peer_block(x, layer, ffn_norm, peer_w_query, peer_sub_keys, peer_u, peer_v, final_norm, *, last):
    pq, hn = norm_matmul(x, ffn_norm[layer], peer_w_query[layer].astype(BF16), tm=1024, tn=512,
                         out_dtype=BF16, emit_xn=True, name="peer_q")
    keys = peer_sub_keys[layer].reshape(2 * PEER_HEADS, PEER_N_KEYS, PEER_HALF).astype(BF16)
    s1, s2, a, b, tau = peer_scores(pq, keys)
    return peer_mix(hn, peer_u[layer].astype(BF16), peer_v[layer].astype(BF16).T, s1, s2, a, b, tau,
                    x, final_norm, final_norm=last)


def kernel(x, mem, positions, ab_norm, ab_w_in, ab_w_out, diff_lq1, diff_lk1, diff_lq2, diff_lk2, diff_subln, s5_a_re, s5_a_im, s5_log_step, s5_b_re, s5_b_im, s5_c_re, s5_c_im, s5_d, s5_w_glu, s5_b_glu, mla_norm, mla_w_in, mla_q_norm, mla_kv_norm, mla_w_uq, mla_w_ukv, mla_w_o, xa_norm, xa_mem_norm, xa_w_q, xa_w_kv, xa_w_o, ffn_norm, peer_w_query, peer_sub_keys, peer_u, peer_v, final_norm):
    batch, seq, D = x.shape
    depth = xa_norm.shape[0]
    x = x.reshape(batch * seq, D)
    mem = mem.reshape(-1, D)
    cos, sin = _rope_tables(positions)
    for layer in range(depth):
        i = layer // 2
        if layer % 2 == 0:
            x = ab_block(x, i, layer, cos, sin, ab_norm, ab_w_in, ab_w_out, diff_lq1, diff_lk1, diff_lq2,
                         diff_lk2, diff_subln, s5_a_re, s5_a_im, s5_log_step, s5_b_re, s5_b_im, s5_c_re,
                         s5_c_im, s5_d, s5_w_glu, s5_b_glu, batch=batch, seq=seq)
        else:
            x = mla_block(x, i, cos, sin, mla_norm, mla_w_in, mla_q_norm, mla_kv_norm, mla_w_uq, mla_w_ukv,
                          mla_w_o, batch=batch, seq=seq)
        x = xa_block(x, mem, layer, xa_norm, xa_mem_norm, xa_w_q, xa_w_kv, xa_w_o, batch=batch, seq=seq)
        x = peer_block(x, layer, ffn_norm, peer_w_query, peer_sub_keys, peer_u, peer_v, final_norm,
                       last=(layer == depth - 1))
    return x.reshape(batch, seq, D)
```

```python
import functools
import math

import jax
import jax.numpy as jnp
from jax import lax
from jax.experimental import pallas as pl
from jax.experimental.pallas import tpu as pltpu

F32 = jnp.float32
BF16 = jnp.bfloat16

NORM_EPS = 1e-6
ROPE_THETA = 10000.0
LANES = 128
SUBLANES = 8
MXU_DEPTH = 256
VMEM_LIMIT = 56 * 1024 * 1024

DIFF_HEADS = 12
DIFF_HEAD_DIM = 64
S5_GROUP = 16
S5_STATE = 64
MLA_HEADS = 16
MLA_Q_RANK = 768
MLA_KV_RANK = 256
MLA_NOPE = 128
MLA_ROPE = 64
MLA_QK_PAD = 256
XA_HEADS = 4
PEER_HEADS = 8
PEER_N_KEYS = 128
PEER_TOPK = 16
PEER_HALF = 128
PEER_TILE_KEYS = 4


def _dot(a, b):
    return jnp.dot(a, b, preferred_element_type=F32)


def _dot_nt(a, b):
    return lax.dot_general(a, b, (((1,), (1,)), ((), ())), preferred_element_type=F32)


def _gelu(x):
    return 0.5 * x * (1.0 + lax.erf(x * (2.0 ** -0.5)))


def _params(*sem):
    return pltpu.CompilerParams(dimension_semantics=sem, vmem_limit_bytes=VMEM_LIMIT)


def _rope_chunk(a, cos, sin, first_half):
    partner = jnp.where(first_half, pltpu.roll(a, LANES - 32, 1), pltpu.roll(a, 32, 1))
    return a * cos + partner * sin


def _norm_matmul_kernel(*refs, rope, emit_xn, tm, tn):
    x_ref, g_ref, w_ref, *rest = refs
    if rope is not None:
        cos_ref, sin_ref, *rest = rest
    o_ref, *rest = rest
    if emit_xn:
        xn_out_ref, *rest = rest
    (xn_ref,) = rest
    j = pl.program_id(1)

    @pl.when(j == 0)
    def _():
        x = x_ref[...].astype(F32)
        ms = jnp.mean(x * x, axis=-1, keepdims=True)
        xn = (x * lax.rsqrt(ms + NORM_EPS) * g_ref[...]).astype(BF16)
        xn_ref[...] = xn
        if emit_xn:
            xn_out_ref[...] = xn

    acc = _dot(xn_ref[...], w_ref[...])
    if rope is None:
        o_ref[...] = acc.astype(o_ref.dtype)
        return
    tile_lo, tile_hi, chunk_mask = rope
    in_range = jnp.logical_and(j >= tile_lo, j < tile_hi)

    @pl.when(in_range)
    def _():
        cos = cos_ref[...]
        sin = sin_ref[...]
        lane = lax.broadcasted_iota(jnp.int32, (tm, LANES), 1)
        first_half = (lane & 63) < 32
        for c in range(tn // LANES):
            a = acc[:, c * LANES:(c + 1) * LANES]
            if chunk_mask[c]:
                a = _rope_chunk(a, cos, sin, first_half)
            o_ref[:, c * LANES:(c + 1) * LANES] = a.astype(o_ref.dtype)

    @pl.when(jnp.logical_not(in_range))
    def _():
        o_ref[...] = acc.astype(o_ref.dtype)


def norm_matmul(x, g, w, *, tm, tn, out_dtype, x_col_block=0, rope=None, cos=None, sin=None,
                emit_xn=False, name=None):
    T = x.shape[0]
    K, N = w.shape
    tm = min(tm, T)
    assert T % tm == 0 and N % tn == 0 and tn % LANES == 0
    in_specs = [pl.BlockSpec((tm, K), lambda i, j: (i, x_col_block)),
                pl.BlockSpec((1, K), lambda i, j: (0, 0)),
                pl.BlockSpec((K, tn), lambda i, j: (0, j))]
    args = [x, g.reshape(1, K).astype(F32), w]
    if rope is not None:
        in_specs += [pl.BlockSpec((tm, LANES), lambda i, j: (i, 0))] * 2
        args += [cos, sin]
    out_shape = [jax.ShapeDtypeStruct((T, N), out_dtype)]
    out_specs = [pl.BlockSpec((tm, tn), lambda i, j: (i, j))]
    if emit_xn:
        out_shape.append(jax.ShapeDtypeStruct((T, K), BF16))
        out_specs.append(pl.BlockSpec((tm, K), lambda i, j: (i, 0)))
    res = pl.pallas_call(
        functools.partial(_norm_matmul_kernel, rope=rope, emit_xn=emit_xn, tm=tm, tn=tn),
        out_shape=out_shape, grid=(T // tm, N // tn), in_specs=in_specs, out_specs=out_specs,
        scratch_shapes=[pltpu.VMEM((tm, K), BF16)],
        compiler_params=_params("parallel", "arbitrary"), name=name)(*args)
    return res if emit_xn else res[0]


def _matmul_residual_kernel(*refs, n_pairs):
    a_refs = refs[:n_pairs]
    w_refs = refs[n_pairs:2 * n_pairs]
    res_ref, o_ref = refs[2 * n_pairs:]
    acc = res_ref[...]
    for a_ref, w_ref in zip(a_refs, w_refs):
        acc = acc + _dot(a_ref[...], w_ref[...])
    o_ref[...] = acc


def matmul_residual(pairs, res, *, tm, tn, name=None):
    T, N = res.shape
    tm = min(tm, T)
    assert T % tm == 0 and N % tn == 0
    in_specs = [pl.BlockSpec((tm, a.shape[1]), lambda i, j: (i, 0)) for a, _ in pairs]
    in_specs += [pl.BlockSpec((w.shape[0], tn), lambda i, j: (0, j)) for _, w in pairs]
    in_specs += [pl.BlockSpec((tm, tn), lambda i, j: (i, j))]
    return pl.pallas_call(
        functools.partial(_matmul_residual_kernel, n_pairs=len(pairs)),
        out_shape=jax.ShapeDtypeStruct((T, N), F32), grid=(T // tm, N // tn),
        in_specs=in_specs, out_specs=pl.BlockSpec((tm, tn), lambda i, j: (i, j)),
        compiler_params=_params("parallel", "arbitrary"), name=name,
    )(*[a for a, _ in pairs], *[w for _, w in pairs], res)


def _flash_q_block(qqs, load_k, load_v, qi, tq, dv):
    n = len(qqs)
    rows = qqs[0].shape[0]

    def kv_step(j, carry, masked):
        out = []
        for hh in range(n):
            m, l, acc = carry[hh]
            s = _dot_nt(qqs[hh], load_k(hh, j))
            if masked:
                row = lax.broadcasted_iota(jnp.int32, (rows, tq), 0)
                col = lax.broadcasted_iota(jnp.int32, (rows, tq), 1)
                s = jnp.where(col <= (row & (tq - 1)), s, -jnp.inf)
            m_new = jnp.maximum(m, jnp.max(s, axis=1, keepdims=True))
            alpha = jnp.exp(m - m_new)
            p = jnp.exp(s - m_new)
            l = alpha * l + jnp.sum(p, axis=1, keepdims=True)
            acc = alpha * acc + _dot(p.astype(BF16), load_v(hh, j))
            out.append((m_new, l, acc))
        return tuple(out)

    init = tuple((jnp.full((rows, 1), -jnp.inf, F32), jnp.zeros((rows, 1), F32),
                  jnp.zeros((rows, dv), F32)) for _ in range(n))
    carry = lax.fori_loop(0, qi, functools.partial(kv_step, masked=False), init)
    carry = kv_step(qi, carry, True)
    return [acc / l for _, l, acc in carry]


def _diff_attn_kernel(lq1_ref, lk1_ref, lq2_ref, lk2_ref, q_ref, k_ref, v_ref, g_ref, o_ref,
                      *, seq, tq, hp, lam_init):
    lam = (jnp.exp(jnp.sum(lq1_ref[...] * lk1_ref[...], keepdims=True))
           - jnp.exp(jnp.sum(lq2_ref[...] * lk2_ref[...], keepdims=True)) + lam_init)
    lane = lax.broadcasted_iota(jnp.int32, (tq, LANES), 1)
    scale = DIFF_HEAD_DIM ** -0.5
    head_lanes = lambda hh: slice(hh * LANES, (hh + 1) * LANES)

    def load_k(hh, j):
        return k_ref[pl.ds(pl.multiple_of(j * tq, tq), tq), head_lanes(hh)]

    def load_v(hh, j):
        return v_ref[pl.ds(pl.multiple_of(j * tq, tq), tq), head_lanes(hh)]

    def q_block(qi, carry):
        r0 = pl.multiple_of(qi * tq, tq)
        qqs = []
        for hh in range(hp):
            qs = q_ref[pl.ds(r0, tq), head_lanes(hh)].astype(F32) * scale
            q1 = jnp.where(lane < DIFF_HEAD_DIM, qs, 0.0).astype(BF16)
            q2 = jnp.where(lane >= DIFF_HEAD_DIM, qs, 0.0).astype(BF16)
            qqs.append(jnp.concatenate([q1, q2], axis=0))
        outs = _flash_q_block(qqs, load_k, load_v, qi, tq, LANES)
        for hh, o in enumerate(outs):
            o = o[:tq] - lam * o[tq:]
            ms = jnp.mean(o * o, axis=-1, keepdims=True)
            o = (o * lax.rsqrt(ms + NORM_EPS) * g_ref[...]) * (1.0 - lam_init)
            o_ref[pl.ds(r0, tq), head_lanes(hh)] = o.astype(o_ref.dtype)
        return carry

    lax.fori_loop(0, seq // tq, q_block, 0)


def diff_attention(proj, lq1, lk1, lq2, lk2, subln, *, batch, seq, lam_init, tq=256, hp=4):
    T = proj.shape[0]
    G = DIFF_HEADS // hp
    W = hp * LANES
    tq = min(tq, seq)
    vec = lambda a: a.reshape(1, -1).astype(F32)
    small = pl.BlockSpec((1, DIFF_HEAD_DIM), lambda b, g: (0, 0))
    return pl.pallas_call(
        functools.partial(_diff_attn_kernel, seq=seq, tq=tq, hp=hp, lam_init=lam_init),
        out_shape=jax.ShapeDtypeStruct((T, DIFF_HEADS * LANES), BF16), grid=(batch, G),
        in_specs=[small, small, small, small,
                  pl.BlockSpec((seq, W), lambda b, g: (b, g)),
                  pl.BlockSpec((seq, W), lambda b, g: (b, G + g)),
                  pl.BlockSpec((seq, W), lambda b, g: (b, 2 * G + g)),
                  pl.BlockSpec((1, LANES), lambda b, g: (0, 0))],
        out_specs=pl.BlockSpec((seq, W), lambda b, g: (b, g)),
        compiler_params=_params("parallel", "parallel"), name="diff_attention",
    )(vec(lq1), vec(lk1), vec(lq2), vec(lk2), proj, proj, proj, vec(subln))


def _mla_attn_kernel(q_ref, kv_ref, kr_ref, o_ref, *, seq, tq, hp, scale):
    def load_k(hh, j):
        rows = pl.ds(pl.multiple_of(j * tq, tq), tq)
        kn = kv_ref[rows, hh * MLA_QK_PAD:hh * MLA_QK_PAD + MLA_NOPE]
        return jnp.concatenate([kn, kr_ref[rows, :]], axis=1)

    def load_v(hh, j):
        rows = pl.ds(pl.multiple_of(j * tq, tq), tq)
        return kv_ref[rows, hh * MLA_QK_PAD + MLA_NOPE:(hh + 1) * MLA_QK_PAD]

    def q_block(qi, carry):
        r0 = pl.multiple_of(qi * tq, tq)
        qqs = [(q_ref[pl.ds(r0, tq), hh * MLA_QK_PAD:(hh + 1) * MLA_QK_PAD].astype(F32) * scale).astype(BF16)
               for hh in range(hp)]
        outs = _flash_q_block(qqs, load_k, load_v, qi, tq, LANES)
        for hh, o in enumerate(outs):
            o_ref[pl.ds(r0, tq), hh * LANES:(hh + 1) * LANES] = o.astype(o_ref.dtype)
        return carry

    lax.fori_loop(0, seq // tq, q_block, 0)


def mla_attention(q, kv, proj, *, batch, seq, tq=256, hp=8):
    T = q.shape[0]
    tq = min(tq, seq)
    kr_block = (MLA_Q_RANK + MLA_KV_RANK) // LANES
    return pl.pallas_call(
        functools.partial(_mla_attn_kernel, seq=seq, tq=tq, hp=hp, scale=(MLA_NOPE + MLA_ROPE) ** -0.5),
        out_shape=jax.ShapeDtypeStruct((T, MLA_HEADS * LANES), BF16), grid=(batch, MLA_HEADS // hp),
        in_specs=[pl.BlockSpec((seq, hp * MLA_QK_PAD), lambda b, g: (b, g)),
                  pl.BlockSpec((seq, hp * MLA_QK_PAD), lambda b, g: (b, g)),
                  pl.BlockSpec((seq, LANES), lambda b, g: (b, kr_block))],
        out_specs=pl.BlockSpec((seq, hp * LANES), lambda b, g: (b, g)),
        compiler_params=_params("parallel", "parallel"), name="mla_attention",
    )(q, kv, proj)


def _xa_kernel(q_ref, k_ref, v_ref, o_ref, *, scale):
    q = (q_ref[...].astype(F32) * scale).astype(BF16)
    s = _dot_nt(q, k_ref[...])
    p = jnp.exp(s - jnp.max(s, axis=1, keepdims=True))
    l = jnp.sum(p, axis=1, keepdims=True)
    o_ref[...] = (_dot(p.astype(BF16), v_ref[...]) / l).astype(o_ref.dtype)


def cross_attention(q, kv, *, batch, seq, mem_len, tq=1024):
    T, D = q.shape
    hd = D // XA_HEADS
    tq = min(tq, seq)
    nq = seq // tq
    return pl.pallas_call(
        functools.partial(_xa_kernel, scale=hd ** -0.5),
        out_shape=jax.ShapeDtypeStruct((T, D), BF16), grid=(batch, XA_HEADS, nq),
        in_specs=[pl.BlockSpec((tq, hd), lambda b, h, i: (b * nq + i, h)),
                  pl.BlockSpec((mem_len, hd), lambda b, h, i: (b, h)),
                  pl.BlockSpec((mem_len, hd), lambda b, h, i: (b, XA_HEADS + h))],
        out_specs=pl.BlockSpec((tq, hd), lambda b, h, i: (b * nq + i, h)),
        compiler_params=_params("parallel", "parallel", "parallel"), name="cross_attention",
    )(q, kv, kv)


def _s5_kernel(u_ref, bmat_ref, are_ref, aim_ref, cmat_ref, d_ref, wglu_ref, bglu_ref, o_ref,
               bu_ref, h_ref, *, batch, tc, ns, lane_chunk):
    @pl.when(pl.program_id(0) == 0)
    def _():
        h_ref[...] = jnp.zeros_like(h_ref)

    u = u_ref[...]
    bu_ref[...] = _dot(u, bmat_ref[...])
    for lc in range(ns // lane_chunk):
        re = slice(lc * lane_chunk, (lc + 1) * lane_chunk)
        im = slice(ns + lc * lane_chunk, ns + (lc + 1) * lane_chunk)
        ar = jnp.broadcast_to(are_ref[:, re], (batch, lane_chunk))
        ai = jnp.broadcast_to(aim_ref[:, re], (batch, lane_chunk))

        def step(t, carry, re=re, im=im, ar=ar, ai=ai):
            hr, hi = carry
            rows = pl.ds(pl.multiple_of(t * batch, batch), batch)
            nr = ar * hr - ai * hi + bu_ref[rows, re]
            ni = ar * hi + ai * hr + bu_ref[rows, im]
            bu_ref[rows, re] = nr
            bu_ref[rows, im] = ni
            return nr, ni

        hr, hi = lax.fori_loop(0, tc, step, (h_ref[:, re], h_ref[:, im]), unroll=4)
        h_ref[:, re] = hr
        h_ref[:, im] = hi
    y = _dot(bu_ref[...].astype(BF16), cmat_ref[...]) + d_ref[...] * u.astype(F32)
    g = _gelu(y)
    z = _dot(g.astype(BF16), wglu_ref[...]) + bglu_ref[...]
    o_ref[...] = (g / (1.0 + jnp.exp(-z))).astype(o_ref.dtype)


def s5_mixer(u_tm, bmat, a_re, a_im, cmat, d, w_glu, b_glu, *, batch, seq, tc=64):
    W = u_tm.shape[1]
    ns = a_re.shape[1]
    tc = min(tc, seq)
    rows = tc * batch
    full = lambda a: pl.BlockSpec(a.shape, lambda t: (0,) * a.ndim)
    args = (bmat, a_re, a_im, cmat, d, w_glu, b_glu)
    return pl.pallas_call(
        functools.partial(_s5_kernel, batch=batch, tc=tc, ns=ns, lane_chunk=min(512, ns)),
        out_shape=jax.ShapeDtypeStruct(u_tm.shape, BF16), grid=(seq // tc,),
        in_specs=[pl.BlockSpec((rows, W), lambda t: (t, 0))] + [full(a) for a in args],
        out_specs=pl.BlockSpec((rows, W), lambda t: (t, 0)),
        scratch_shapes=[pltpu.VMEM((rows, 2 * ns), F32), pltpu.VMEM((batch, 2 * ns), F32)],
        compiler_params=_params("arbitrary"), name="s5_mixer",
    )(u_tm, *args)


def _top_sorted(s, k):
    rows = []
    for _ in range(k):
        mx = jnp.max(s, axis=0, keepdims=True)
        rows.append(mx)
        s = jnp.where(s >= mx, -jnp.inf, s)
    return jnp.concatenate(rows, axis=0)


def _peer_score_kernel(q_ref, keys_ref, s1_ref, s2_ref, a_ref, b_ref, tau_ref, *, tm, n_i):
    s1 = _dot_nt(keys_ref[0], q_ref[:, :PEER_HALF])
    s2 = _dot_nt(keys_ref[1], q_ref[:, PEER_HALF:])
    s2_ref[...] = s2
    for grp in range(PEER_N_KEYS // n_i):
        s1_ref[grp] = s1[grp * n_i:(grp + 1) * n_i, :]
    K = PEER_TOPK
    for c in range(tm // LANES):
        lanes = slice(c * LANES, (c + 1) * LANES)
        x1 = s1[:, lanes]
        x2 = s2[:, lanes]
        t1 = _top_sorted(x1, K)
        t2 = _top_sorted(x2, K)
        cands = [t1[0:1] + t2]
        cands += [t1[i:i + 1] + t2[0:SUBLANES] for i in range(1, SUBLANES)]
        cands += [t1[SUBLANES:K] + t2[0:1]]
        cand = jnp.concatenate(cands, axis=0)
        best = _top_sorted(cand, K)
        tau = best[K - 1:K]
        m = best[0:1]
        z = jnp.sum(jnp.where(cand >= tau, jnp.exp(cand - m), 0.0), axis=0, keepdims=True)
        tau_ref[:, lanes] = tau
        a = jnp.exp(x1 - t1[0:1]) / z
        for grp in range(PEER_N_KEYS // n_i):
            a_ref[grp, :, lanes] = a[grp * n_i:(grp + 1) * n_i, :]
        b_ref[:, lanes] = jnp.exp(x2 - t2[0:1])


def peer_scores(q, keys, *, n_i, tm=512):
    T = q.shape[0]
    tm = min(tm, T)
    H = PEER_HEADS
    n_grp = PEER_N_KEYS // n_i
    big = jax.ShapeDtypeStruct((H, PEER_N_KEYS, T), F32)
    big_spec = pl.BlockSpec((None, PEER_N_KEYS, tm), lambda i, h: (h, 0, i))
    grp = jax.ShapeDtypeStruct((H, n_grp, n_i, T), F32)
    grp_spec = pl.BlockSpec((None, n_grp, n_i, tm), lambda i, h: (h, 0, 0, i))
    return pl.pallas_call(
        functools.partial(_peer_score_kernel, tm=tm, n_i=n_i),
        out_shape=[grp, big, grp, big, jax.ShapeDtypeStruct((H, 1, T), F32)],
        grid=(T // tm, H),
        in_specs=[pl.BlockSpec((tm, 2 * PEER_HALF), lambda i, h: (i, h)),
                  pl.BlockSpec((2, PEER_N_KEYS, PEER_HALF), lambda i, h: (h, 0, 0))],
        out_specs=[grp_spec, big_spec, grp_spec, big_spec,
                   pl.BlockSpec((None, 1, tm), lambda i, h: (h, 0, i))],
        compiler_params=_params("parallel", "parallel"), name="peer_scores",
    )(q, keys)


def _gate_group(s1_ref, s2_ref, a_ref, b_ref, tau_ref, c, jb, n_i):
    heads = range(PEER_HEADS)
    lanes = slice(c * LANES, (c + 1) * LANES)
    rows = slice(jb * SUBLANES, (jb + 1) * SUBLANES)
    tau = [tau_ref[h, :, lanes] for h in heads]
    s2 = [s2_ref[h, rows, lanes] for h in heads]
    b = [b_ref[h, rows, lanes] for h in heads]
    out = []
    for ii in range(n_i):
        terms = [jnp.where(s1_ref[h, ii:ii + 1, lanes] + s2[h] >= tau[h],
                           a_ref[h, ii:ii + 1, lanes] * b[h], 0.0) for h in heads]
        while len(terms) > 1:
            terms = [x + y for x, y in zip(terms[0::2], terms[1::2])]
        out.append(terms[0])
    return out


def _exact_zero(tiles):
    while len(tiles) > 2:
        tiles = [x + y for x, y in zip(tiles[0::2], tiles[1::2])]
    if len(tiles) == 1:
        tiles = tiles * 2
    zero = [pltpu.bitcast(lax.shift_right_logical(pltpu.bitcast(t, jnp.uint32), jnp.uint32(32)), F32)
            for t in tiles]
    return jnp.concatenate(zero, axis=0).astype(BF16)


def _anchor(x, zeros):
    pack = 2 * SUBLANES
    rows_out = []
    for r in range(x.shape[0] // pack):
        row = x[r * pack:(r + 1) * pack, :]
        hits = sorted(((l0, z) for (rr, l0), z in zeros.items() if rr == r), key=lambda t: t[0])
        if hits:
            parts, pos = [], 0
            for l0, z in hits:
                if l0 > pos:
                    parts.append(row[:, pos:l0])
                parts.append(row[:, l0:l0 + LANES] + z)
                pos = l0 + LANES
            if pos < x.shape[1]:
                parts.append(row[:, pos:])
            row = jnp.concatenate(parts, axis=1)
        rows_out.append(row)
    return jnp.concatenate(rows_out, axis=0)


def _peer_mix_kernel(h_ref, u_ref, vt_ref, s1_ref, s1n_ref, s2_ref, a_ref, an_ref, b_ref, tau_ref, x_ref,
                     g_ref, o_ref, acc_ref, ht_ref, gs_ref, *, tm, te, final_norm):
    j = pl.program_id(1)
    n_i = te // PEER_N_KEYS
    n_c = tm // LANES
    n_jb = PEER_N_KEYS // SUBLANES
    c_lo = n_c // 2
    pack = 2 * SUBLANES

    def gates_lo(s1r, ar):
        out = []
        for c in range(c_lo):
            for jb in range(n_jb):
                tiles = _gate_group(s1r, s2_ref, ar, b_ref, tau_ref, c, jb, n_i)
                for ii, t in enumerate(tiles):
                    r0 = ii * PEER_N_KEYS + jb * SUBLANES
                    gs_ref[r0:r0 + SUBLANES, c * LANES:(c + 1) * LANES] = t
                out.append(tiles)
        return out

    @pl.when(j == 0)
    def _():
        acc_ref[...] = jnp.zeros_like(acc_ref)
        ht_ref[...] = h_ref[...].T
        gates_lo(s1_ref, a_ref)

    g_lo = gs_ref[...]
    u = u_ref[...]
    k_blocks = u.shape[1] // MXU_DEPTH
    n_r = te // pack
    groups = [(c, jb) for c in range(c_lo, n_c) for jb in range(n_jb)]
    g = [[[None] * n_c for _ in range(n_jb)] for _ in range(n_i)]
    zeros = {}
    for q, (c, jb) in enumerate(groups):
        tiles = _gate_group(s1_ref, s2_ref, a_ref, b_ref, tau_ref, c, jb, n_i)
        for ii, t in enumerate(tiles):
            g[ii][jb][c] = t
        r = (q * n_r) // len(groups)
        zeros[(r, ((r * k_blocks) // n_r) * MXU_DEPTH)] = _exact_zero(tiles)
    st = _dot(_anchor(u, zeros), ht_ref[...])
    g_hi = jnp.concatenate([jnp.concatenate(g[ii][jb][c_lo:], axis=1)
                            for ii in range(n_i) for jb in range(n_jb)], axis=0)
    wt = (jnp.concatenate([g_lo, g_hi], axis=1) * _gelu(st)).astype(BF16)

    vt = vt_ref[...]
    n_rv = vt.shape[0] // pack
    kb_v = te // MXU_DEPTH
    nxt = gates_lo(s1n_ref, an_ref)
    zeros = {}
    for q, tiles in enumerate(nxt):
        pos = (q * n_rv * kb_v) // len(nxt)
        zeros[(pos % n_rv, (pos // n_rv) * MXU_DEPTH)] = _exact_zero(tiles)
    acc_ref[...] += _dot(_anchor(vt, zeros), wt)

    @pl.when(j == pl.num_programs(1) - 1)
    def _():
        y = x_ref[...] + acc_ref[...].T
        if final_norm:
            ms = jnp.mean(y * y, axis=-1, keepdims=True)
            y = y * lax.rsqrt(ms + NORM_EPS) * g_ref[...]
        o_ref[...] = y


def peer_mix(h, u, vt, s1, s2, a, b, tau, x, g_final, *, final_norm, tm=512):
    T, D = x.shape
    E = u.shape[0]
    tm = min(tm, T)
    H = PEER_HEADS
    tok = pl.BlockSpec((tm, D), lambda i, j: (i, 0))
    n_i = s1.shape[2]
    te = n_i * PEER_N_KEYS
    ne = E // te
    sc = pl.BlockSpec((H, PEER_N_KEYS, tm), lambda i, j: (0, 0, i))
    grp = pl.BlockSpec((H, None, n_i, tm), lambda i, j: (0, j, 0, i))
    grp_next = pl.BlockSpec((H, None, n_i, tm), lambda i, j: (0, jnp.minimum(j + 1, ne - 1), 0, i))
    return pl.pallas_call(
        functools.partial(_peer_mix_kernel, tm=tm, te=te, final_norm=final_norm),
        out_shape=jax.ShapeDtypeStruct((T, D), F32), grid=(T // tm, ne),
        in_specs=[tok,
                  pl.BlockSpec((te, D), lambda i, j: (j, 0)),
                  pl.BlockSpec((D, te), lambda i, j: (0, j)),
                  grp, grp_next, sc, grp, grp_next, sc,
                  pl.BlockSpec((H, 1, tm), lambda i, j: (0, 0, i)),
                  tok,
                  pl.BlockSpec((1, D), lambda i, j: (0, 0))],
        out_specs=tok,
        scratch_shapes=[pltpu.VMEM((D, tm), F32), pltpu.VMEM((D, tm), BF16),
                        pltpu.VMEM((te, tm // 2), F32)],
        compiler_params=_params("arbitrary", "arbitrary"), name="peer_mix",
    )(h, u, vt, s1, s1, s2, a, a, b, tau, x, g_final.reshape(1, D).astype(F32))


def _rope_tables(positions):
    half = DIFF_HEAD_DIM // 2
    inv_freq = ROPE_THETA ** (-jnp.arange(half, dtype=F32) * 2.0 / DIFF_HEAD_DIM)
    ang = positions.astype(F32).reshape(-1, 1) * inv_freq
    cos, sin = jnp.cos(ang), jnp.sin(ang)
    return jnp.tile(cos, (1, 4)), jnp.concatenate([-sin, sin, -sin, sin], axis=1)


def _s5_params(a_re, a_im, log_step, b_re, b_im, c_re, c_im):
    G, N = a_re.shape
    P = b_re.shape[-1]
    step = jnp.exp(log_step.astype(F32))[:, None]
    mag = jnp.exp(a_re * step)
    ab_re = mag * jnp.cos(a_im * step)
    ab_im = mag * jnp.sin(a_im * step)
    den = a_re * a_re + a_im * a_im
    num_re = ab_re - 1.0
    f_re = (num_re * a_re + ab_im * a_im) / den
    f_im = (ab_im * a_re - num_re * a_im) / den
    bb_re = f_re[..., None] * b_re - f_im[..., None] * b_im
    bb_im = f_re[..., None] * b_im + f_im[..., None] * b_re
    eye = jnp.eye(G, dtype=F32)
    blk_in = lambda m: jnp.einsum('gnp,gh->gphn', m, eye).reshape(G * P, G * N)
    blk_out = lambda m: jnp.einsum('gpn,gh->gnhp', m, eye).reshape(G * N, G * P)
    bmat = jnp.concatenate([blk_in(bb_re), blk_in(bb_im)], axis=1).astype(BF16)
    cmat = jnp.concatenate([blk_out(c_re), -blk_out(c_im)], axis=0).astype(BF16)
    return bmat, ab_re.reshape(1, G * N), ab_im.reshape(1, G * N), cmat


def ab_block(x, i, layer, cos, sin, ab_norm, ab_w_in, ab_w_out, diff_lq1, diff_lk1, diff_lq2, diff_lk2,
             diff_subln, s5_a_re, s5_a_im, s5_log_step, s5_b_re, s5_b_im, s5_c_re, s5_c_im, s5_d,
             s5_w_glu, s5_b_glu, *, batch, seq):
    T = x.shape[0]
    diff_w = DIFF_HEADS * 2 * DIFF_HEAD_DIM
    s5_w = ab_w_in.shape[2] - 3 * diff_w
    lam_init = 0.8 - 0.6 * math.exp(-0.3 * layer)
    proj = norm_matmul(x, ab_norm[i], ab_w_in[i].astype(BF16), tm=1024, tn=512, out_dtype=BF16,
                       rope=(0, 2 * diff_w // 512, (True,) * 4), cos=cos, sin=sin, name="ab_in")
    a_out = diff_attention(proj, diff_lq1[i], diff_lk1[i], diff_lq2[i], diff_lk2[i], diff_subln[i],
                           batch=batch, seq=seq, lam_init=lam_init)
    bmat, a_re, a_im, cmat = _s5_params(s5_a_re[i], s5_a_im[i], s5_log_step[i], s5_b_re[i], s5_b_im[i],
                                        s5_c_re[i], s5_c_im[i])
    u_tm = proj[:, 3 * diff_w:].reshape(batch, seq, s5_w).transpose(1, 0, 2).reshape(T, s5_w)
    b_tm = s5_mixer(u_tm, bmat, a_re, a_im, cmat, s5_d[i].reshape(1, s5_w).astype(F32),
                    s5_w_glu[i].astype(BF16), s5_b_glu[i].reshape(1, s5_w).astype(F32),
                    batch=batch, seq=seq)
    b_out = b_tm.reshape(seq, batch, s5_w).transpose(1, 0, 2).reshape(T, s5_w)
    w_out = ab_w_out[i].astype(BF16)
    return matmul_residual([(a_out, w_out[:diff_w]), (b_out, w_out[diff_w:])], x, tm=1024, tn=512,
                           name="ab_out")


def mla_block(x, i, cos, sin, mla_norm, mla_w_in, mla_q_norm, mla_kv_norm, mla_w_uq, mla_w_ukv, mla_w_o,
              *, batch, seq):
    in_w = mla_w_in.shape[2]
    in_pad = (MLA_Q_RANK + MLA_KV_RANK) + LANES
    w_in = jnp.pad(mla_w_in[i], ((0, 0), (0, in_pad - in_w))).astype(BF16)
    proj = norm_matmul(x, mla_norm[i], w_in, tm=1024, tn=in_pad // 3, out_dtype=BF16,
                       rope=(2, 3, (False, False, True)), cos=cos, sin=sin, name="mla_in")
    w_uq = mla_w_uq[i].reshape(MLA_Q_RANK, MLA_HEADS, MLA_NOPE + MLA_ROPE)
    w_uq = jnp.pad(w_uq, ((0, 0), (0, 0), (0, MLA_QK_PAD - MLA_NOPE - MLA_ROPE)))
    w_uq = w_uq.reshape(MLA_Q_RANK, MLA_HEADS * MLA_QK_PAD).astype(BF16)
    q = norm_matmul(proj, mla_q_norm[i], w_uq, tm=1024, tn=512, out_dtype=BF16, x_col_block=0,
                    rope=(0, MLA_HEADS * MLA_QK_PAD // 512, (False, True, False, True)),
                    cos=cos, sin=sin, name="mla_q")
    kv = norm_matmul(proj, mla_kv_norm[i], mla_w_ukv[i].astype(BF16), tm=1024, tn=512, out_dtype=BF16,
                     x_col_block=MLA_Q_RANK // MLA_KV_RANK, name="mla_kv")
    o = mla_attention(q, kv, proj, batch=batch, seq=seq)
    return matmul_residual([(o, mla_w_o[i].astype(BF16))], x, tm=1024, tn=512, name="mla_out")


def xa_block(x, mem, layer, xa_norm, xa_mem_norm, xa_w_q, xa_w_kv, xa_w_o, *, batch, seq):
    mem_len = mem.shape[0] // batch
    q = norm_matmul(x, xa_norm[layer], xa_w_q[layer].astype(BF16), tm=1024, tn=512, out_dtype=BF16,
                    name="xa_q")
    kv = norm_matmul(mem, xa_mem_norm[layer], xa_w_kv[layer].astype(BF16), tm=1024, tn=512,
                     out_dtype=BF16, name="xa_kv")
    o = cross_attention(q, kv, batch=batch, seq=seq, mem_len=mem_len)
    return matmul_residual([(o, xa_w_o[layer].astype(BF16))], x, tm=1024, tn=512, name="xa_out")


def peer_block(x, layer, ffn_norm, peer_w_query, peer_sub_keys, peer_u, peer_v, final_norm, *, last):
    pq, hn = norm_matmul(x, ffn_norm[layer], peer_w_query[layer].astype(BF16), tm=1024, tn=512,
                         out_dtype=BF16, emit_xn=True, name="peer_q")
    keys = peer_sub_keys[layer].reshape(2 * PEER_HEADS, PEER_N_KEYS, PEER_HALF).astype(BF16)
    s1, s2, a, b, tau = peer_scores(pq, keys, n_i=PEER_TILE_KEYS)
    return peer_mix(hn, peer_u[layer].astype(BF16), peer_v[layer].astype(BF16).T, s1, s2, a, b, tau,
                    x, final_norm, final_norm=last)


def kernel(x, mem, positions, ab_norm, ab_w_in, ab_w_out, diff_lq1, diff_lk1, diff_lq2, diff_lk2, diff_subln, s5_a_re, s5_a_im, s5_log_step, s5_b_re, s5_b_im, s5_c_re, s5_c_im, s5_d, s5_w_glu, s5_b_glu, mla_norm, mla_w_in, mla_q_norm, mla_kv_norm, mla_w_uq, mla_w_ukv, mla_w_o, xa_norm, xa_mem_norm, xa_w_q, xa_w_kv, xa_w_o, ffn_norm, peer_w_query, peer_sub_keys, peer_u, peer_v, final_norm):
    batch, seq, D = x.shape
    depth = xa_norm.shape[0]
    x = x.reshape(batch * seq, D)
    mem = mem.reshape(-1, D)
    cos, sin = _rope_tables(positions)
    for layer in range(depth):
        i = layer // 2
        if layer % 2 == 0:
            x = ab_block(x, i, layer, cos, sin, ab_norm, ab_w_in, ab_w_out, diff_lq1, diff_lk1, diff_lq2,
                         diff_lk2, diff_subln, s5_a_re, s5_a_im, s5_log_step, s5_b_re, s5_b_im, s5_c_re,
                         s5_c_im, s5_d, s5_w_glu, s5_b_glu, batch=batch, seq=seq)
        else:
            x = mla_block(x, i, cos, sin, mla_norm, mla_w_in, mla_q_norm, mla_kv_norm, mla_w_uq, mla_w_ukv,
                          mla_w_o, batch=batch, seq=seq)
        x = xa_block(x, mem, layer, xa_norm, xa_mem_norm, xa_w_q, xa_w_kv, xa_w_o, batch=batch, seq=seq)
        x = peer_block(x, layer, ffn_norm, peer_w_query, peer_sub_keys, peer_u, peer_v, final_norm,
                       last=(layer == depth - 1))
    return x.reshape(batch, seq, D)
```

```python
import functools
import math

import jax
import jax.numpy as jnp
from jax import lax
from jax.experimental import pallas as pl
from jax.experimental.pallas import tpu as pltpu

F32 = jnp.float32
BF16 = jnp.bfloat16

NORM_EPS = 1e-6
ROPE_THETA = 10000.0
LANES = 128
SUBLANES = 8
MXU_DEPTH = 256
TM_RESIDENT = 512
TM_STREAMED = 1024
TN_STREAMED = 512
VMEM_LIMIT = 56 * 1024 * 1024

DIFF_HEADS = 12
DIFF_HEAD_DIM = 64
S5_GROUP = 16
S5_STATE = 64
MLA_HEADS = 16
MLA_Q_RANK = 768
MLA_KV_RANK = 256
MLA_NOPE = 128
MLA_ROPE = 64
MLA_QK_PAD = 256
XA_HEADS = 4
PEER_HEADS = 8
PEER_N_KEYS = 128
PEER_TOPK = 16
PEER_HALF = 128
PEER_TILE_KEYS = 4


def _dot(a, b):
    return jnp.dot(a, b, preferred_element_type=F32)


def _dot_nt(a, b):
    return lax.dot_general(a, b, (((1,), (1,)), ((), ())), preferred_element_type=F32)


def _gelu(x):
    return 0.5 * x * (1.0 + lax.erf(x * (2.0 ** -0.5)))


def _params(*sem):
    return pltpu.CompilerParams(dimension_semantics=sem, vmem_limit_bytes=VMEM_LIMIT)


def _rope_chunk(a, cos, sin, first_half):
    partner = jnp.where(first_half, pltpu.roll(a, LANES - 32, 1), pltpu.roll(a, 32, 1))
    return a * cos + partner * sin


def _norm_matmul_kernel(*refs, rope, emit_xn, tm, tn):
    x_ref, g_ref, w_ref, *rest = refs
    if rope is not None:
        cos_ref, sin_ref, *rest = rest
    o_ref, *rest = rest
    if emit_xn:
        xn_out_ref, *rest = rest
    (xn_ref,) = rest
    j = pl.program_id(1)

    @pl.when(j == 0)
    def _():
        x = x_ref[...].astype(F32)
        ms = jnp.mean(x * x, axis=-1, keepdims=True)
        xn = (x * lax.rsqrt(ms + NORM_EPS) * g_ref[...]).astype(BF16)
        xn_ref[...] = xn
        if emit_xn:
            xn_out_ref[...] = xn

    acc = _dot(xn_ref[...], w_ref[...])
    if rope is None:
        o_ref[...] = acc.astype(o_ref.dtype)
        return
    tile_lo, tile_hi, chunk_mask = rope
    in_range = jnp.logical_and(j >= tile_lo, j < tile_hi)

    @pl.when(in_range)
    def _():
        cos = cos_ref[...]
        sin = sin_ref[...]
        lane = lax.broadcasted_iota(jnp.int32, (tm, LANES), 1)
        first_half = (lane & 63) < 32
        for c in range(tn // LANES):
            a = acc[:, c * LANES:(c + 1) * LANES]
            if chunk_mask[c]:
                a = _rope_chunk(a, cos, sin, first_half)
            o_ref[:, c * LANES:(c + 1) * LANES] = a.astype(o_ref.dtype)

    @pl.when(jnp.logical_not(in_range))
    def _():
        o_ref[...] = acc.astype(o_ref.dtype)


def norm_matmul(x, g, w, *, tm, tn, out_dtype, x_col_block=0, rope=None, cos=None, sin=None,
                emit_xn=False, name=None):
    T = x.shape[0]
    K, N = w.shape
    tm = min(tm, T)
    assert T % tm == 0 and N % tn == 0 and tn % LANES == 0
    in_specs = [pl.BlockSpec((tm, K), lambda i, j: (i, x_col_block)),
                pl.BlockSpec((1, K), lambda i, j: (0, 0)),
                pl.BlockSpec((K, tn), lambda i, j: (0, j))]
    args = [x, g.reshape(1, K).astype(F32), w]
    if rope is not None:
        in_specs += [pl.BlockSpec((tm, LANES), lambda i, j: (i, 0))] * 2
        args += [cos, sin]
    out_shape = [jax.ShapeDtypeStruct((T, N), out_dtype)]
    out_specs = [pl.BlockSpec((tm, tn), lambda i, j: (i, j))]
    if emit_xn:
        out_shape.append(jax.ShapeDtypeStruct((T, K), BF16))
        out_specs.append(pl.BlockSpec((tm, K), lambda i, j: (i, 0)))
    res = pl.pallas_call(
        functools.partial(_norm_matmul_kernel, rope=rope, emit_xn=emit_xn, tm=tm, tn=tn),
        out_shape=out_shape, grid=(T // tm, N // tn), in_specs=in_specs, out_specs=out_specs,
        scratch_shapes=[pltpu.VMEM((tm, K), BF16)],
        compiler_params=_params("parallel", "arbitrary"), name=name)(*args)
    return res if emit_xn else res[0]


def _matmul_residual_kernel(*refs, n_pairs):
    a_refs = refs[:n_pairs]
    w_refs = refs[n_pairs:2 * n_pairs]
    res_ref, o_ref = refs[2 * n_pairs:]
    acc = res_ref[...]
    for a_ref, w_ref in zip(a_refs, w_refs):
        acc = acc + _dot(a_ref[...], w_ref[...])
    o_ref[...] = acc


def matmul_residual(pairs, res, *, tm, tn, name=None):
    T, N = res.shape
    tm = min(tm, T)
    assert T % tm == 0 and N % tn == 0
    in_specs = [pl.BlockSpec((tm, a.shape[1]), lambda i, j: (i, 0)) for a, _ in pairs]
    in_specs += [pl.BlockSpec((w.shape[0], tn), lambda i, j: (0, j)) for _, w in pairs]
    in_specs += [pl.BlockSpec((tm, tn), lambda i, j: (i, j))]
    return pl.pallas_call(
        functools.partial(_matmul_residual_kernel, n_pairs=len(pairs)),
        out_shape=jax.ShapeDtypeStruct((T, N), F32), grid=(T // tm, N // tn),
        in_specs=in_specs, out_specs=pl.BlockSpec((tm, tn), lambda i, j: (i, j)),
        compiler_params=_params("parallel", "arbitrary"), name=name,
    )(*[a for a, _ in pairs], *[w for _, w in pairs], res)


def _flash_q_block(qqs, load_k, load_v, qi, tq, dv):
    n = len(qqs)
    rows = qqs[0].shape[0]

    def kv_step(j, carry, masked):
        out = []
        for hh in range(n):
            m, l, acc = carry[hh]
            s = _dot_nt(qqs[hh], load_k(hh, j))
            if masked:
                row = lax.broadcasted_iota(jnp.int32, (rows, tq), 0)
                col = lax.broadcasted_iota(jnp.int32, (rows, tq), 1)
                s = jnp.where(col <= (row & (tq - 1)), s, -jnp.inf)
            m_new = jnp.maximum(m, jnp.max(s, axis=1, keepdims=True))
            alpha = jnp.exp(m - m_new)
            p = jnp.exp(s - m_new)
            l = alpha * l + jnp.sum(p, axis=1, keepdims=True)
            acc = alpha * acc + _dot(p.astype(BF16), load_v(hh, j))
            out.append((m_new, l, acc))
        return tuple(out)

    init = tuple((jnp.full((rows, 1), -jnp.inf, F32), jnp.zeros((rows, 1), F32),
                  jnp.zeros((rows, dv), F32)) for _ in range(n))
    carry = lax.fori_loop(0, qi, functools.partial(kv_step, masked=False), init)
    carry = kv_step(qi, carry, True)
    return [acc / l for _, l, acc in carry]


def _diff_attn_kernel(lq1_ref, lk1_ref, lq2_ref, lk2_ref, q_ref, k_ref, v_ref, g_ref, o_ref,
                      *, seq, tq, hp, lam_init):
    lam = (jnp.exp(jnp.sum(lq1_ref[...] * lk1_ref[...], keepdims=True))
           - jnp.exp(jnp.sum(lq2_ref[...] * lk2_ref[...], keepdims=True)) + lam_init)
    lane = lax.broadcasted_iota(jnp.int32, (tq, LANES), 1)
    scale = DIFF_HEAD_DIM ** -0.5
    head_lanes = lambda hh: slice(hh * LANES, (hh + 1) * LANES)

    def load_k(hh, j):
        return k_ref[pl.ds(pl.multiple_of(j * tq, tq), tq), head_lanes(hh)]

    def load_v(hh, j):
        return v_ref[pl.ds(pl.multiple_of(j * tq, tq), tq), head_lanes(hh)]

    def q_block(qi, carry):
        r0 = pl.multiple_of(qi * tq, tq)
        qqs = []
        for hh in range(hp):
            qs = q_ref[pl.ds(r0, tq), head_lanes(hh)].astype(F32) * scale
            q1 = jnp.where(lane < DIFF_HEAD_DIM, qs, 0.0).astype(BF16)
            q2 = jnp.where(lane >= DIFF_HEAD_DIM, qs, 0.0).astype(BF16)
            qqs.append(jnp.concatenate([q1, q2], axis=0))
        outs = _flash_q_block(qqs, load_k, load_v, qi, tq, LANES)
        for hh, o in enumerate(outs):
            o = o[:tq] - lam * o[tq:]
            ms = jnp.mean(o * o, axis=-1, keepdims=True)
            o = (o * lax.rsqrt(ms + NORM_EPS) * g_ref[...]) * (1.0 - lam_init)
            o_ref[pl.ds(r0, tq), head_lanes(hh)] = o.astype(o_ref.dtype)
        return carry

    lax.fori_loop(0, seq // tq, q_block, 0)


def diff_attention(proj, lq1, lk1, lq2, lk2, subln, *, batch, seq, lam_init, tq=256, hp=4):
    T = proj.shape[0]
    G = DIFF_HEADS // hp
    W = hp * LANES
    tq = min(tq, seq)
    vec = lambda a: a.reshape(1, -1).astype(F32)
    small = pl.BlockSpec((1, DIFF_HEAD_DIM), lambda b, g: (0, 0))
    return pl.pallas_call(
        functools.partial(_diff_attn_kernel, seq=seq, tq=tq, hp=hp, lam_init=lam_init),
        out_shape=jax.ShapeDtypeStruct((T, DIFF_HEADS * LANES), BF16), grid=(batch, G),
        in_specs=[small, small, small, small,
                  pl.BlockSpec((seq, W), lambda b, g: (b, g)),
                  pl.BlockSpec((seq, W), lambda b, g: (b, G + g)),
                  pl.BlockSpec((seq, W), lambda b, g: (b, 2 * G + g)),
                  pl.BlockSpec((1, LANES), lambda b, g: (0, 0))],
        out_specs=pl.BlockSpec((seq, W), lambda b, g: (b, g)),
        compiler_params=_params("parallel", "parallel"), name="diff_attention",
    )(vec(lq1), vec(lk1), vec(lq2), vec(lk2), proj, proj, proj, vec(subln))


def _mla_attn_kernel(q_ref, kv_ref, kr_ref, o_ref, *, seq, tq, hp, scale):
    def load_k(hh, j):
        rows = pl.ds(pl.multiple_of(j * tq, tq), tq)
        kn = kv_ref[rows, hh * MLA_QK_PAD:hh * MLA_QK_PAD + MLA_NOPE]
        return jnp.concatenate([kn, kr_ref[rows, :]], axis=1)

    def load_v(hh, j):
        rows = pl.ds(pl.multiple_of(j * tq, tq), tq)
        return kv_ref[rows, hh * MLA_QK_PAD + MLA_NOPE:(hh + 1) * MLA_QK_PAD]

    def q_block(qi, carry):
        r0 = pl.multiple_of(qi * tq, tq)
        qqs = [(q_ref[pl.ds(r0, tq), hh * MLA_QK_PAD:(hh + 1) * MLA_QK_PAD].astype(F32) * scale).astype(BF16)
               for hh in range(hp)]
        outs = _flash_q_block(qqs, load_k, load_v, qi, tq, LANES)
        for hh, o in enumerate(outs):
            o_ref[pl.ds(r0, tq), hh * LANES:(hh + 1) * LANES] = o.astype(o_ref.dtype)
        return carry

    lax.fori_loop(0, seq // tq, q_block, 0)


def mla_attention(q, kv, proj, *, batch, seq, tq=256, hp=8):
    T = q.shape[0]
    tq = min(tq, seq)
    kr_block = (MLA_Q_RANK + MLA_KV_RANK) // LANES
    return pl.pallas_call(
        functools.partial(_mla_attn_kernel, seq=seq, tq=tq, hp=hp, scale=(MLA_NOPE + MLA_ROPE) ** -0.5),
        out_shape=jax.ShapeDtypeStruct((T, MLA_HEADS * LANES), BF16), grid=(batch, MLA_HEADS // hp),
        in_specs=[pl.BlockSpec((seq, hp * MLA_QK_PAD), lambda b, g: (b, g)),
                  pl.BlockSpec((seq, hp * MLA_QK_PAD), lambda b, g: (b, g)),
                  pl.BlockSpec((seq, LANES), lambda b, g: (b, kr_block))],
        out_specs=pl.BlockSpec((seq, hp * LANES), lambda b, g: (b, g)),
        compiler_params=_params("parallel", "parallel"), name="mla_attention",
    )(q, kv, proj)


def _xa_kernel(q_ref, k_ref, v_ref, o_ref, *, scale):
    q = (q_ref[...].astype(F32) * scale).astype(BF16)
    s = _dot_nt(q, k_ref[...])
    p = jnp.exp(s - jnp.max(s, axis=1, keepdims=True))
    l = jnp.sum(p, axis=1, keepdims=True)
    o_ref[...] = (_dot(p.astype(BF16), v_ref[...]) / l).astype(o_ref.dtype)


def cross_attention(q, kv, *, batch, seq, mem_len, tq=1024):
    T, D = q.shape
    hd = D // XA_HEADS
    tq = min(tq, seq)
    nq = seq // tq
    return pl.pallas_call(
        functools.partial(_xa_kernel, scale=hd ** -0.5),
        out_shape=jax.ShapeDtypeStruct((T, D), BF16), grid=(batch, XA_HEADS, nq),
        in_specs=[pl.BlockSpec((tq, hd), lambda b, h, i: (b * nq + i, h)),
                  pl.BlockSpec((mem_len, hd), lambda b, h, i: (b, h)),
                  pl.BlockSpec((mem_len, hd), lambda b, h, i: (b, XA_HEADS + h))],
        out_specs=pl.BlockSpec((tq, hd), lambda b, h, i: (b * nq + i, h)),
        compiler_params=_params("parallel", "parallel", "parallel"), name="cross_attention",
    )(q, kv, kv)


def _s5_kernel(u_ref, bmat_ref, are_ref, aim_ref, cmat_ref, d_ref, wglu_ref, bglu_ref, o_ref,
               bu_ref, h_ref, *, batch, tc, ns, lane_chunk):
    @pl.when(pl.program_id(0) == 0)
    def _():
        h_ref[...] = jnp.zeros_like(h_ref)

    u = u_ref[...]
    bu_ref[...] = _dot(u, bmat_ref[...])
    for lc in range(ns // lane_chunk):
        re = slice(lc * lane_chunk, (lc + 1) * lane_chunk)
        im = slice(ns + lc * lane_chunk, ns + (lc + 1) * lane_chunk)
        ar = jnp.broadcast_to(are_ref[:, re], (batch, lane_chunk))
        ai = jnp.broadcast_to(aim_ref[:, re], (batch, lane_chunk))

        def step(t, carry, re=re, im=im, ar=ar, ai=ai):
            hr, hi = carry
            rows = pl.ds(pl.multiple_of(t * batch, batch), batch)
            nr = ar * hr - ai * hi + bu_ref[rows, re]
            ni = ar * hi + ai * hr + bu_ref[rows, im]
            bu_ref[rows, re] = nr
            bu_ref[rows, im] = ni
            return nr, ni

        hr, hi = lax.fori_loop(0, tc, step, (h_ref[:, re], h_ref[:, im]), unroll=4)
        h_ref[:, re] = hr
        h_ref[:, im] = hi
    y = _dot(bu_ref[...].astype(BF16), cmat_ref[...]) + d_ref[...] * u.astype(F32)
    g = _gelu(y)
    z = _dot(g.astype(BF16), wglu_ref[...]) + bglu_ref[...]
    o_ref[...] = (g / (1.0 + jnp.exp(-z))).astype(o_ref.dtype)


def s5_mixer(u_tm, bmat, a_re, a_im, cmat, d, w_glu, b_glu, *, batch, seq, tc=64):
    W = u_tm.shape[1]
    ns = a_re.shape[1]
    tc = min(tc, seq)
    rows = tc * batch
    full = lambda a: pl.BlockSpec(a.shape, lambda t: (0,) * a.ndim)
    args = (bmat, a_re, a_im, cmat, d, w_glu, b_glu)
    return pl.pallas_call(
        functools.partial(_s5_kernel, batch=batch, tc=tc, ns=ns, lane_chunk=min(512, ns)),
        out_shape=jax.ShapeDtypeStruct(u_tm.shape, BF16), grid=(seq // tc,),
        in_specs=[pl.BlockSpec((rows, W), lambda t: (t, 0))] + [full(a) for a in args],
        out_specs=pl.BlockSpec((rows, W), lambda t: (t, 0)),
        scratch_shapes=[pltpu.VMEM((rows, 2 * ns), F32), pltpu.VMEM((batch, 2 * ns), F32)],
        compiler_params=_params("arbitrary"), name="s5_mixer",
    )(u_tm, *args)


def _top_sorted(s, k):
    rows = []
    for _ in range(k):
        mx = jnp.max(s, axis=0, keepdims=True)
        rows.append(mx)
        s = jnp.where(s >= mx, -jnp.inf, s)
    return jnp.concatenate(rows, axis=0)


def _peer_score_kernel(q_ref, keys_ref, s1_ref, s2_ref, a_ref, b_ref, tau_ref, *, tm, n_i):
    s1 = _dot_nt(keys_ref[0], q_ref[:, :PEER_HALF])
    s2 = _dot_nt(keys_ref[1], q_ref[:, PEER_HALF:])
    s2_ref[...] = s2
    for grp in range(PEER_N_KEYS // n_i):
        s1_ref[grp] = s1[grp * n_i:(grp + 1) * n_i, :]
    K = PEER_TOPK
    for c in range(tm // LANES):
        lanes = slice(c * LANES, (c + 1) * LANES)
        x1 = s1[:, lanes]
        x2 = s2[:, lanes]
        t1 = _top_sorted(x1, K)
        t2 = _top_sorted(x2, K)
        cands = [t1[0:1] + t2]
        cands += [t1[i:i + 1] + t2[0:SUBLANES] for i in range(1, SUBLANES)]
        cands += [t1[SUBLANES:K] + t2[0:1]]
        cand = jnp.concatenate(cands, axis=0)
        best = _top_sorted(cand, K)
        tau = best[K - 1:K]
        m = best[0:1]
        z = jnp.sum(jnp.where(cand >= tau, jnp.exp(cand - m), 0.0), axis=0, keepdims=True)
        tau_ref[:, lanes] = tau
        a = jnp.exp(x1 - t1[0:1]) / z
        for grp in range(PEER_N_KEYS // n_i):
            a_ref[grp, :, lanes] = a[grp * n_i:(grp + 1) * n_i, :]
        b_ref[:, lanes] = jnp.exp(x2 - t2[0:1])


def peer_scores(q, keys, *, n_i, tm=512):
    T = q.shape[0]
    tm = min(tm, T)
    H = PEER_HEADS
    n_grp = PEER_N_KEYS // n_i
    big = jax.ShapeDtypeStruct((H, PEER_N_KEYS, T), F32)
    big_spec = pl.BlockSpec((None, PEER_N_KEYS, tm), lambda i, h: (h, 0, i))
    grp = jax.ShapeDtypeStruct((H, n_grp, n_i, T), F32)
    grp_spec = pl.BlockSpec((None, n_grp, n_i, tm), lambda i, h: (h, 0, 0, i))
    return pl.pallas_call(
        functools.partial(_peer_score_kernel, tm=tm, n_i=n_i),
        out_shape=[grp, big, grp, big, jax.ShapeDtypeStruct((H, 1, T), F32)],
        grid=(T // tm, H),
        in_specs=[pl.BlockSpec((tm, 2 * PEER_HALF), lambda i, h: (i, h)),
                  pl.BlockSpec((2, PEER_N_KEYS, PEER_HALF), lambda i, h: (h, 0, 0))],
        out_specs=[grp_spec, big_spec, grp_spec, big_spec,
                   pl.BlockSpec((None, 1, tm), lambda i, h: (h, 0, i))],
        compiler_params=_params("parallel", "parallel"), name="peer_scores",
    )(q, keys)


def _gate_group(s1_ref, s2_ref, a_ref, b_ref, tau_ref, c, jb, n_i):
    heads = range(PEER_HEADS)
    lanes = slice(c * LANES, (c + 1) * LANES)
    rows = slice(jb * SUBLANES, (jb + 1) * SUBLANES)
    tau = [tau_ref[h, :, lanes] for h in heads]
    s2 = [s2_ref[h, rows, lanes] for h in heads]
    b = [b_ref[h, rows, lanes] for h in heads]
    out = []
    for ii in range(n_i):
        terms = [jnp.where(s1_ref[h, ii:ii + 1, lanes] + s2[h] >= tau[h],
                           a_ref[h, ii:ii + 1, lanes] * b[h], 0.0) for h in heads]
        while len(terms) > 1:
            terms = [x + y for x, y in zip(terms[0::2], terms[1::2])]
        out.append(terms[0])
    return out


def _exact_zero(tiles):
    while len(tiles) > 2:
        tiles = [x + y for x, y in zip(tiles[0::2], tiles[1::2])]
    if len(tiles) == 1:
        tiles = tiles * 2
    zero = [pltpu.bitcast(lax.shift_right_logical(pltpu.bitcast(t, jnp.uint32), jnp.uint32(32)), F32)
            for t in tiles]
    return jnp.concatenate(zero, axis=0).astype(BF16)


def _anchor(x, zeros):
    pack = 2 * SUBLANES
    rows_out = []
    for r in range(x.shape[0] // pack):
        row = x[r * pack:(r + 1) * pack, :]
        hits = sorted(((l0, z) for (rr, l0), z in zeros.items() if rr == r), key=lambda t: t[0])
        if hits:
            parts, pos = [], 0
            for l0, z in hits:
                if l0 > pos:
                    parts.append(row[:, pos:l0])
                parts.append(row[:, l0:l0 + LANES] + z)
                pos = l0 + LANES
            if pos < x.shape[1]:
                parts.append(row[:, pos:])
            row = jnp.concatenate(parts, axis=1)
        rows_out.append(row)
    return jnp.concatenate(rows_out, axis=0)


def _peer_mix_kernel(h_ref, u_ref, vt_ref, s1_ref, s1n_ref, s2_ref, a_ref, an_ref, b_ref, tau_ref, x_ref,
                     g_ref, o_ref, acc_ref, ht_ref, gs_ref, *, tm, te, final_norm):
    j = pl.program_id(1)
    n_i = te // PEER_N_KEYS
    n_c = tm // LANES
    n_jb = PEER_N_KEYS // SUBLANES
    c_lo = n_c // 2
    pack = 2 * SUBLANES

    def gates_lo(s1r, ar):
        out = []
        for c in range(c_lo):
            for jb in range(n_jb):
                tiles = _gate_group(s1r, s2_ref, ar, b_ref, tau_ref, c, jb, n_i)
                for ii, t in enumerate(tiles):
                    r0 = ii * PEER_N_KEYS + jb * SUBLANES
                    gs_ref[r0:r0 + SUBLANES, c * LANES:(c + 1) * LANES] = t
                out.append(tiles)
        return out

    @pl.when(j == 0)
    def _():
        acc_ref[...] = jnp.zeros_like(acc_ref)
        ht_ref[...] = h_ref[...].T
        gates_lo(s1_ref, a_ref)

    g_lo = gs_ref[...]
    u = u_ref[...]
    k_blocks = u.shape[1] // MXU_DEPTH
    n_r = te // pack
    groups = [(c, jb) for c in range(c_lo, n_c) for jb in range(n_jb)]
    g = [[[None] * n_c for _ in range(n_jb)] for _ in range(n_i)]
    zeros = {}
    for q, (c, jb) in enumerate(groups):
        tiles = _gate_group(s1_ref, s2_ref, a_ref, b_ref, tau_ref, c, jb, n_i)
        for ii, t in enumerate(tiles):
            g[ii][jb][c] = t
        r = (q * n_r) // len(groups)
        zeros[(r, ((r * k_blocks) // n_r) * MXU_DEPTH)] = _exact_zero(tiles)
    st = _dot(_anchor(u, zeros), ht_ref[...])
    g_hi = jnp.concatenate([jnp.concatenate(g[ii][jb][c_lo:], axis=1)
                            for ii in range(n_i) for jb in range(n_jb)], axis=0)
    wt = (jnp.concatenate([g_lo, g_hi], axis=1) * _gelu(st)).astype(BF16)

    vt = vt_ref[...]
    n_rv = vt.shape[0] // pack
    kb_v = te // MXU_DEPTH
    nxt = gates_lo(s1n_ref, an_ref)
    zeros = {}
    for q, tiles in enumerate(nxt):
        pos = (q * n_rv * kb_v) // len(nxt)
        zeros[(pos % n_rv, (pos // n_rv) * MXU_DEPTH)] = _exact_zero(tiles)
    acc_ref[...] += _dot(_anchor(vt, zeros), wt)

    @pl.when(j == pl.num_programs(1) - 1)
    def _():
        y = x_ref[...] + acc_ref[...].T
        if final_norm:
            ms = jnp.mean(y * y, axis=-1, keepdims=True)
            y = y * lax.rsqrt(ms + NORM_EPS) * g_ref[...]
        o_ref[...] = y


def peer_mix(h, u, vt, s1, s2, a, b, tau, x, g_final, *, final_norm, tm=512):
    T, D = x.shape
    E = u.shape[0]
    tm = min(tm, T)
    H = PEER_HEADS
    tok = pl.BlockSpec((tm, D), lambda i, j: (i, 0))
    n_i = s1.shape[2]
    te = n_i * PEER_N_KEYS
    ne = E // te
    sc = pl.BlockSpec((H, PEER_N_KEYS, tm), lambda i, j: (0, 0, i))
    grp = pl.BlockSpec((H, None, n_i, tm), lambda i, j: (0, j, 0, i))
    grp_next = pl.BlockSpec((H, None, n_i, tm), lambda i, j: (0, jnp.minimum(j + 1, ne - 1), 0, i))
    return pl.pallas_call(
        functools.partial(_peer_mix_kernel, tm=tm, te=te, final_norm=final_norm),
        out_shape=jax.ShapeDtypeStruct((T, D), F32), grid=(T // tm, ne),
        in_specs=[tok,
                  pl.BlockSpec((te, D), lambda i, j: (j, 0)),
                  pl.BlockSpec((None, D, te), lambda i, j: (j, 0, 0)),
                  grp, grp_next, sc, grp, grp_next, sc,
                  pl.BlockSpec((H, 1, tm), lambda i, j: (0, 0, i)),
                  tok,
                  pl.BlockSpec((1, D), lambda i, j: (0, 0))],
        out_specs=tok,
        scratch_shapes=[pltpu.VMEM((D, tm), F32), pltpu.VMEM((D, tm), BF16),
                        pltpu.VMEM((te, tm // 2), F32)],
        compiler_params=_params("arbitrary", "arbitrary"), name="peer_mix",
    )(h, u, vt, s1, s1, s2, a, a, b, tau, x, g_final.reshape(1, D).astype(F32))


def _rope_tables(positions):
    half = DIFF_HEAD_DIM // 2
    inv_freq = ROPE_THETA ** (-jnp.arange(half, dtype=F32) * 2.0 / DIFF_HEAD_DIM)
    ang = positions.astype(F32).reshape(-1, 1) * inv_freq
    cos, sin = jnp.cos(ang), jnp.sin(ang)
    return jnp.tile(cos, (1, 4)), jnp.concatenate([-sin, sin, -sin, sin], axis=1)


def _s5_params(a_re, a_im, log_step, b_re, b_im, c_re, c_im):
    G, N = a_re.shape
    P = b_re.shape[-1]
    step = jnp.exp(log_step.astype(F32))[:, None]
    mag = jnp.exp(a_re * step)
    ab_re = mag * jnp.cos(a_im * step)
    ab_im = mag * jnp.sin(a_im * step)
    den = a_re * a_re + a_im * a_im
    num_re = ab_re - 1.0
    f_re = (num_re * a_re + ab_im * a_im) / den
    f_im = (ab_im * a_re - num_re * a_im) / den
    bb_re = f_re[..., None] * b_re - f_im[..., None] * b_im
    bb_im = f_re[..., None] * b_im + f_im[..., None] * b_re
    eye = jnp.eye(G, dtype=F32)
    blk_in = lambda m: jnp.einsum('gnp,gh->gphn', m, eye).reshape(G * P, G * N)
    blk_out = lambda m: jnp.einsum('gpn,gh->gnhp', m, eye).reshape(G * N, G * P)
    bmat = jnp.concatenate([blk_in(bb_re), blk_in(bb_im)], axis=1).astype(BF16)
    cmat = jnp.concatenate([blk_out(c_re), -blk_out(c_im)], axis=0).astype(BF16)
    return bmat, ab_re.reshape(1, G * N), ab_im.reshape(1, G * N), cmat


def ab_block(x, i, layer, cos, sin, ab_norm, ab_w_in, ab_w_out, diff_lq1, diff_lk1, diff_lq2, diff_lk2,
             diff_subln, s5_a_re, s5_a_im, s5_log_step, s5_b_re, s5_b_im, s5_c_re, s5_c_im, s5_d,
             s5_w_glu, s5_b_glu, *, batch, seq):
    T = x.shape[0]
    diff_w = DIFF_HEADS * 2 * DIFF_HEAD_DIM
    s5_w = ab_w_in.shape[2] - 3 * diff_w
    lam_init = 0.8 - 0.6 * math.exp(-0.3 * layer)
    proj = norm_matmul(x, ab_norm[i], ab_w_in[i].astype(BF16), tm=TM_STREAMED, tn=TN_STREAMED, out_dtype=BF16,
                       rope=(0, 2 * diff_w // TN_STREAMED, (True,) * (TN_STREAMED // LANES)), cos=cos, sin=sin,
                       name="ab_in")
    a_out = diff_attention(proj, diff_lq1[i], diff_lk1[i], diff_lq2[i], diff_lk2[i], diff_subln[i],
                           batch=batch, seq=seq, lam_init=lam_init)
    bmat, a_re, a_im, cmat = _s5_params(s5_a_re[i], s5_a_im[i], s5_log_step[i], s5_b_re[i], s5_b_im[i],
                                        s5_c_re[i], s5_c_im[i])
    u_tm = proj[:, 3 * diff_w:].reshape(batch, seq, s5_w).transpose(1, 0, 2).reshape(T, s5_w)
    b_tm = s5_mixer(u_tm, bmat, a_re, a_im, cmat, s5_d[i].reshape(1, s5_w).astype(F32),
                    s5_w_glu[i].astype(BF16), s5_b_glu[i].reshape(1, s5_w).astype(F32),
                    batch=batch, seq=seq)
    b_out = b_tm.reshape(seq, batch, s5_w).transpose(1, 0, 2).reshape(T, s5_w)
    w_out = ab_w_out[i].astype(BF16)
    return matmul_residual([(a_out, w_out[:diff_w]), (b_out, w_out[diff_w:])], x, tm=TM_RESIDENT,
                           tn=x.shape[1], name="ab_out")


def mla_block(x, i, cos, sin, mla_norm, mla_w_in, mla_q_norm, mla_kv_norm, mla_w_uq, mla_w_ukv, mla_w_o,
              *, batch, seq):
    in_w = mla_w_in.shape[2]
    in_pad = (MLA_Q_RANK + MLA_KV_RANK) + LANES
    w_in = jnp.pad(mla_w_in[i], ((0, 0), (0, in_pad - in_w))).astype(BF16)
    proj = norm_matmul(x, mla_norm[i], w_in, tm=TM_RESIDENT, tn=in_pad, out_dtype=BF16,
                       rope=(0, 1, (False,) * (in_pad // LANES - 1) + (True,)), cos=cos, sin=sin, name="mla_in")
    w_uq = mla_w_uq[i].reshape(MLA_Q_RANK, MLA_HEADS, MLA_NOPE + MLA_ROPE)
    w_uq = jnp.pad(w_uq, ((0, 0), (0, 0), (0, MLA_QK_PAD - MLA_NOPE - MLA_ROPE)))
    w_uq = w_uq.reshape(MLA_Q_RANK, MLA_HEADS * MLA_QK_PAD).astype(BF16)
    q = norm_matmul(proj, mla_q_norm[i], w_uq, tm=TM_RESIDENT, tn=w_uq.shape[1], out_dtype=BF16, x_col_block=0,
                    rope=(0, 1, (False, True) * MLA_HEADS), cos=cos, sin=sin, name="mla_q")
    kv = norm_matmul(proj, mla_kv_norm[i], mla_w_ukv[i].astype(BF16), tm=TM_RESIDENT, tn=mla_w_ukv.shape[2],
                     out_dtype=BF16, x_col_block=MLA_Q_RANK // MLA_KV_RANK, name="mla_kv")
    o = mla_attention(q, kv, proj, batch=batch, seq=seq)
    return matmul_residual([(o, mla_w_o[i].astype(BF16))], x, tm=TM_RESIDENT, tn=x.shape[1], name="mla_out")


def xa_block(x, mem, layer, xa_norm, xa_mem_norm, xa_w_q, xa_w_kv, xa_w_o, *, batch, seq):
    mem_len = mem.shape[0] // batch
    q = norm_matmul(x, xa_norm[layer], xa_w_q[layer].astype(BF16), tm=TM_RESIDENT, tn=x.shape[1],
                    out_dtype=BF16, name="xa_q")
    kv = norm_matmul(mem, xa_mem_norm[layer], xa_w_kv[layer].astype(BF16), tm=TM_STREAMED, tn=TN_STREAMED,
                     out_dtype=BF16, name="xa_kv")
    o = cross_attention(q, kv, batch=batch, seq=seq, mem_len=mem_len)
    return matmul_residual([(o, xa_w_o[layer].astype(BF16))], x, tm=TM_RESIDENT, tn=x.shape[1], name="xa_out")


def peer_block(x, layer, ffn_norm, peer_w_query, peer_sub_keys, peer_u, peer_v, final_norm, *, last):
    pq, hn = norm_matmul(x, ffn_norm[layer], peer_w_query[layer].astype(BF16), tm=TM_RESIDENT,
                         tn=peer_w_query.shape[2], out_dtype=BF16, emit_xn=True, name="peer_q")
    keys = peer_sub_keys[layer].reshape(2 * PEER_HEADS, PEER_N_KEYS, PEER_HALF).astype(BF16)
    s1, s2, a, b, tau = peer_scores(pq, keys, n_i=PEER_TILE_KEYS)
    te = PEER_TILE_KEYS * PEER_N_KEYS
    vt = peer_v[layer].astype(BF16).reshape(-1, te, x.shape[1]).transpose(0, 2, 1)
    return peer_mix(hn, peer_u[layer].astype(BF16), vt, s1, s2, a, b, tau,
                    x, final_norm, final_norm=last)


def kernel(x, mem, positions, ab_norm, ab_w_in, ab_w_out, diff_lq1, diff_lk1, diff_lq2, diff_lk2, diff_subln, s5_a_re, s5_a_im, s5_log_step, s5_b_re, s5_b_im, s5_c_re, s5_c_im, s5_d, s5_w_glu, s5_b_glu, mla_norm, mla_w_in, mla_q_norm, mla_kv_norm, mla_w_uq, mla_w_ukv, mla_w_o, xa_norm, xa_mem_norm, xa_w_q, xa_w_kv, xa_w_o, ffn_norm, peer_w_query, peer_sub_keys, peer_u, peer_v, final_norm):
    batch, seq, D = x.shape
    depth = xa_norm.shape[0]
    x = x.reshape(batch * seq, D)
    mem = mem.reshape(-1, D)
    cos, sin = _rope_tables(positions)
    for layer in range(depth):
        i = layer // 2
        if layer % 2 == 0:
            x = ab_block(x, i, layer, cos, sin, ab_norm, ab_w_in, ab_w_out, diff_lq1, diff_lk1, diff_lq2,
                         diff_lk2, diff_subln, s5_a_re, s5_a_im, s5_log_step, s5_b_re, s5_b_im, s5_c_re,
                         s5_c_im, s5_d, s5_w_glu, s5_b_glu, batch=batch, seq=seq)
        else:
            x = mla_block(x, i, cos, sin, mla_norm, mla_w_in, mla_q_norm, mla_kv_norm, mla_w_uq, mla_w_ukv,
                          mla_w_o, batch=batch, seq=seq)
        x = xa_block(x, mem, layer, xa_norm, xa_mem_norm, xa_w_q, xa_w_kv, xa_w_o, batch=batch, seq=seq)
        x = peer_block(x, layer, ffn_norm, peer_w_query, peer_sub_keys, peer_u, peer_v, final_norm,
                       last=(layer == depth - 1))
    return x.reshape(batch, seq, D)
```

```python
import functools
import math

import jax
import jax.numpy as jnp
from jax import lax
from jax.experimental import pallas as pl
from jax.experimental.pallas import tpu as pltpu

F32 = jnp.float32
BF16 = jnp.bfloat16

NORM_EPS = 1e-6
ROPE_THETA = 10000.0
LANES = 128
SUBLANES = 8
MXU_DEPTH = 256
TM_RESIDENT = 512
TM_STREAMED = 1024
TN_STREAMED = 512
VMEM_LIMIT = 56 * 1024 * 1024

DIFF_HEADS = 12
DIFF_HEAD_DIM = 64
S5_GROUP = 16
S5_STATE = 64
MLA_HEADS = 16
MLA_Q_RANK = 768
MLA_KV_RANK = 256
MLA_NOPE = 128
MLA_ROPE = 64
MLA_QK_PAD = 256
XA_HEADS = 4
PEER_HEADS = 8
PEER_N_KEYS = 128
PEER_TOPK = 16
PEER_HALF = 128
PEER_TILE_KEYS = 4


def _dot(a, b):
    return jnp.dot(a, b, preferred_element_type=F32)


def _dot_nt(a, b):
    return lax.dot_general(a, b, (((1,), (1,)), ((), ())), preferred_element_type=F32)


def _gelu(x):
    return 0.5 * x * (1.0 + lax.erf(x * (2.0 ** -0.5)))


def _params(*sem):
    return pltpu.CompilerParams(dimension_semantics=sem, vmem_limit_bytes=VMEM_LIMIT)


def _rope_chunk(a, cos, sin, first_half):
    partner = jnp.where(first_half, pltpu.roll(a, LANES - 32, 1), pltpu.roll(a, 32, 1))
    return a * cos + partner * sin


def _norm_matmul_kernel(*refs, rope, emit_xn, tm, tn):
    x_ref, g_ref, w_ref, *rest = refs
    if rope is not None:
        cos_ref, sin_ref, *rest = rest
    o_ref, *rest = rest
    if emit_xn:
        xn_out_ref, *rest = rest
    (xn_ref,) = rest
    j = pl.program_id(1)

    @pl.when(j == 0)
    def _():
        x = x_ref[...].astype(F32)
        ms = jnp.mean(x * x, axis=-1, keepdims=True)
        xn = (x * lax.rsqrt(ms + NORM_EPS) * g_ref[...]).astype(BF16)
        xn_ref[...] = xn
        if emit_xn:
            xn_out_ref[...] = xn

    acc = _dot(xn_ref[...], w_ref[...])
    if rope is None:
        o_ref[...] = acc.astype(o_ref.dtype)
        return
    tile_lo, tile_hi, chunk_mask = rope
    in_range = jnp.logical_and(j >= tile_lo, j < tile_hi)

    @pl.when(in_range)
    def _():
        cos = cos_ref[...]
        sin = sin_ref[...]
        lane = lax.broadcasted_iota(jnp.int32, (tm, LANES), 1)
        first_half = (lane & 63) < 32
        for c in range(tn // LANES):
            a = acc[:, c * LANES:(c + 1) * LANES]
            if chunk_mask[c]:
                a = _rope_chunk(a, cos, sin, first_half)
            o_ref[:, c * LANES:(c + 1) * LANES] = a.astype(o_ref.dtype)

    @pl.when(jnp.logical_not(in_range))
    def _():
        o_ref[...] = acc.astype(o_ref.dtype)


def norm_matmul(x, g, w, *, tm, tn, out_dtype, x_col_block=0, rope=None, cos=None, sin=None,
                emit_xn=False, name=None):
    T = x.shape[0]
    K, N = w.shape
    tm = min(tm, T)
    assert T % tm == 0 and N % tn == 0 and tn % LANES == 0
    in_specs = [pl.BlockSpec((tm, K), lambda i, j: (i, x_col_block)),
                pl.BlockSpec((1, K), lambda i, j: (0, 0)),
                pl.BlockSpec((K, tn), lambda i, j: (0, j))]
    args = [x, g.reshape(1, K).astype(F32), w]
    if rope is not None:
        in_specs += [pl.BlockSpec((tm, LANES), lambda i, j: (i, 0))] * 2
        args += [cos, sin]
    out_shape = [jax.ShapeDtypeStruct((T, N), out_dtype)]
    out_specs = [pl.BlockSpec((tm, tn), lambda i, j: (i, j))]
    if emit_xn:
        out_shape.append(jax.ShapeDtypeStruct((T, K), BF16))
        out_specs.append(pl.BlockSpec((tm, K), lambda i, j: (i, 0)))
    res = pl.pallas_call(
        functools.partial(_norm_matmul_kernel, rope=rope, emit_xn=emit_xn, tm=tm, tn=tn),
        out_shape=out_shape, grid=(T // tm, N // tn), in_specs=in_specs, out_specs=out_specs,
        scratch_shapes=[pltpu.VMEM((tm, K), BF16)],
        compiler_params=_params("parallel", "arbitrary"), name=name)(*args)
    return res if emit_xn else res[0]


def _matmul_residual_kernel(*refs, n_pairs):
    a_refs = refs[:n_pairs]
    w_refs = refs[n_pairs:2 * n_pairs]
    res_ref, o_ref = refs[2 * n_pairs:]
    acc = res_ref[...]
    for a_ref, w_ref in zip(a_refs, w_refs):
        acc = acc + _dot(a_ref[...], w_ref[...])
    o_ref[...] = acc


def matmul_residual(pairs, res, *, tm, tn, name=None):
    T, N = res.shape
    tm = min(tm, T)
    assert T % tm == 0 and N % tn == 0
    in_specs = [pl.BlockSpec((tm, a.shape[1]), lambda i, j: (i, 0)) for a, _ in pairs]
    in_specs += [pl.BlockSpec((w.shape[0], tn), lambda i, j: (0, j)) for _, w in pairs]
    in_specs += [pl.BlockSpec((tm, tn), lambda i, j: (i, j))]
    return pl.pallas_call(
        functools.partial(_matmul_residual_kernel, n_pairs=len(pairs)),
        out_shape=jax.ShapeDtypeStruct((T, N), F32), grid=(T // tm, N // tn),
        in_specs=in_specs, out_specs=pl.BlockSpec((tm, tn), lambda i, j: (i, j)),
        compiler_params=_params("parallel", "arbitrary"), name=name,
    )(*[a for a, _ in pairs], *[w for _, w in pairs], res)


def _flash_q_block(qqs, load_k, load_v, qi, tq, dv):
    n = len(qqs)
    rows = qqs[0].shape[0]

    def kv_step(j, carry, masked):
        out = []
        for hh in range(n):
            m, l, acc = carry[hh]
            s = _dot_nt(qqs[hh], load_k(hh, j))
            if masked:
                row = lax.broadcasted_iota(jnp.int32, (rows, tq), 0)
                col = lax.broadcasted_iota(jnp.int32, (rows, tq), 1)
                s = jnp.where(col <= (row & (tq - 1)), s, -jnp.inf)
            m_new = jnp.maximum(m, jnp.max(s, axis=1, keepdims=True))
            alpha = jnp.exp(m - m_new)
            p = jnp.exp(s - m_new)
            l = alpha * l + jnp.sum(p, axis=1, keepdims=True)
            acc = alpha * acc + _dot(p.astype(BF16), load_v(hh, j))
            out.append((m_new, l, acc))
        return tuple(out)

    init = tuple((jnp.full((rows, 1), -jnp.inf, F32), jnp.zeros((rows, 1), F32),
                  jnp.zeros((rows, dv), F32)) for _ in range(n))
    carry = lax.fori_loop(0, qi, functools.partial(kv_step, masked=False), init)
    carry = kv_step(qi, carry, True)
    return [acc / l for _, l, acc in carry]


def _diff_attn_kernel(lq1_ref, lk1_ref, lq2_ref, lk2_ref, q_ref, k_ref, v_ref, g_ref, o_ref,
                      *, seq, tq, hp, lam_init):
    lam = (jnp.exp(jnp.sum(lq1_ref[...] * lk1_ref[...], keepdims=True))
           - jnp.exp(jnp.sum(lq2_ref[...] * lk2_ref[...], keepdims=True)) + lam_init)
    lane = lax.broadcasted_iota(jnp.int32, (tq, LANES), 1)
    scale = DIFF_HEAD_DIM ** -0.5
    head_lanes = lambda hh: slice(hh * LANES, (hh + 1) * LANES)

    def load_k(hh, j):
        return k_ref[pl.ds(pl.multiple_of(j * tq, tq), tq), head_lanes(hh)]

    def load_v(hh, j):
        return v_ref[pl.ds(pl.multiple_of(j * tq, tq), tq), head_lanes(hh)]

    def q_block(qi, carry):
        r0 = pl.multiple_of(qi * tq, tq)
        qqs = []
        for hh in range(hp):
            qs = q_ref[pl.ds(r0, tq), head_lanes(hh)].astype(F32) * scale
            q1 = jnp.where(lane < DIFF_HEAD_DIM, qs, 0.0).astype(BF16)
            q2 = jnp.where(lane >= DIFF_HEAD_DIM, qs, 0.0).astype(BF16)
            qqs.append(jnp.concatenate([q1, q2], axis=0))
        outs = _flash_q_block(qqs, load_k, load_v, qi, tq, LANES)
        for hh, o in enumerate(outs):
            o = o[:tq] - lam * o[tq:]
            ms = jnp.mean(o * o, axis=-1, keepdims=True)
            o = (o * lax.rsqrt(ms + NORM_EPS) * g_ref[...]) * (1.0 - lam_init)
            o_ref[pl.ds(r0, tq), head_lanes(hh)] = o.astype(o_ref.dtype)
        return carry

    lax.fori_loop(0, seq // tq, q_block, 0)


def diff_attention(proj, lq1, lk1, lq2, lk2, subln, *, batch, seq, lam_init, tq=512, hp=2):
    T = proj.shape[0]
    G = DIFF_HEADS // hp
    W = hp * LANES
    tq = min(tq, seq)
    vec = lambda a: a.reshape(1, -1).astype(F32)
    small = pl.BlockSpec((1, DIFF_HEAD_DIM), lambda b, g: (0, 0))
    return pl.pallas_call(
        functools.partial(_diff_attn_kernel, seq=seq, tq=tq, hp=hp, lam_init=lam_init),
        out_shape=jax.ShapeDtypeStruct((T, DIFF_HEADS * LANES), BF16), grid=(batch, G),
        in_specs=[small, small, small, small,
                  pl.BlockSpec((seq, W), lambda b, g: (b, g)),
                  pl.BlockSpec((seq, W), lambda b, g: (b, G + g)),
                  pl.BlockSpec((seq, W), lambda b, g: (b, 2 * G + g)),
                  pl.BlockSpec((1, LANES), lambda b, g: (0, 0))],
        out_specs=pl.BlockSpec((seq, W), lambda b, g: (b, g)),
        compiler_params=_params("parallel", "parallel"), name="diff_attention",
    )(vec(lq1), vec(lk1), vec(lq2), vec(lk2), proj, proj, proj, vec(subln))


def _mla_attn_kernel(q_ref, kv_ref, kr_ref, o_ref, *, seq, tq, hp, scale):
    def load_k(hh, j):
        rows = pl.ds(pl.multiple_of(j * tq, tq), tq)
        kn = kv_ref[rows, hh * MLA_QK_PAD:hh * MLA_QK_PAD + MLA_NOPE]
        return jnp.concatenate([kn, kr_ref[rows, :]], axis=1)

    def load_v(hh, j):
        rows = pl.ds(pl.multiple_of(j * tq, tq), tq)
        return kv_ref[rows, hh * MLA_QK_PAD + MLA_NOPE:(hh + 1) * MLA_QK_PAD]

    def q_block(qi, carry):
        r0 = pl.multiple_of(qi * tq, tq)
        qqs = [(q_ref[pl.ds(r0, tq), hh * MLA_QK_PAD:(hh + 1) * MLA_QK_PAD].astype(F32) * scale).astype(BF16)
               for hh in range(hp)]
        outs = _flash_q_block(qqs, load_k, load_v, qi, tq, LANES)
        for hh, o in enumerate(outs):
            o_ref[pl.ds(r0, tq), hh * LANES:(hh + 1) * LANES] = o.astype(o_ref.dtype)
        return carry

    lax.fori_loop(0, seq // tq, q_block, 0)


def mla_attention(q, kv, proj, *, batch, seq, tq=512, hp=4):
    T = q.shape[0]
    tq = min(tq, seq)
    kr_block = (MLA_Q_RANK + MLA_KV_RANK) // LANES
    return pl.pallas_call(
        functools.partial(_mla_attn_kernel, seq=seq, tq=tq, hp=hp, scale=(MLA_NOPE + MLA_ROPE) ** -0.5),
        out_shape=jax.ShapeDtypeStruct((T, MLA_HEADS * LANES), BF16), grid=(batch, MLA_HEADS // hp),
        in_specs=[pl.BlockSpec((seq, hp * MLA_QK_PAD), lambda b, g: (b, g)),
                  pl.BlockSpec((seq, hp * MLA_QK_PAD), lambda b, g: (b, g)),
                  pl.BlockSpec((seq, LANES), lambda b, g: (b, kr_block))],
        out_specs=pl.BlockSpec((seq, hp * LANES), lambda b, g: (b, g)),
        compiler_params=_params("parallel", "parallel"), name="mla_attention",
    )(q, kv, proj)


def _xa_kernel(q_ref, k_ref, v_ref, o_ref, *, scale):
    q = (q_ref[...].astype(F32) * scale).astype(BF16)
    s = _dot_nt(q, k_ref[...])
    p = jnp.exp(s - jnp.max(s, axis=1, keepdims=True))
    l = jnp.sum(p, axis=1, keepdims=True)
    o_ref[...] = (_dot(p.astype(BF16), v_ref[...]) / l).astype(o_ref.dtype)


def cross_attention(q, kv, *, batch, seq, mem_len, tq=1024):
    T, D = q.shape
    hd = D // XA_HEADS
    tq = min(tq, seq)
    nq = seq // tq
    return pl.pallas_call(
        functools.partial(_xa_kernel, scale=hd ** -0.5),
        out_shape=jax.ShapeDtypeStruct((T, D), BF16), grid=(batch, XA_HEADS, nq),
        in_specs=[pl.BlockSpec((tq, hd), lambda b, h, i: (b * nq + i, h)),
                  pl.BlockSpec((mem_len, hd), lambda b, h, i: (b, h)),
                  pl.BlockSpec((mem_len, hd), lambda b, h, i: (b, XA_HEADS + h))],
        out_specs=pl.BlockSpec((tq, hd), lambda b, h, i: (b * nq + i, h)),
        compiler_params=_params("parallel", "parallel", "parallel"), name="cross_attention",
    )(q, kv, kv)


def _s5_kernel(u_ref, bmat_ref, are_ref, aim_ref, cmat_ref, d_ref, wglu_ref, bglu_ref, o_ref,
               bu_ref, h_ref, *, batch, tc, ns, lane_chunk):
    @pl.when(pl.program_id(0) == 0)
    def _():
        h_ref[...] = jnp.zeros_like(h_ref)

    u = u_ref[...]
    bu_ref[...] = _dot(u, bmat_ref[...])
    for lc in range(ns // lane_chunk):
        re = slice(lc * lane_chunk, (lc + 1) * lane_chunk)
        im = slice(ns + lc * lane_chunk, ns + (lc + 1) * lane_chunk)
        ar = jnp.broadcast_to(are_ref[:, re], (batch, lane_chunk))
        ai = jnp.broadcast_to(aim_ref[:, re], (batch, lane_chunk))

        def step(t, carry, re=re, im=im, ar=ar, ai=ai):
            hr, hi = carry
            rows = pl.ds(pl.multiple_of(t * batch, batch), batch)
            nr = ar * hr - ai * hi + bu_ref[rows, re]
            ni = ar * hi + ai * hr + bu_ref[rows, im]
            bu_ref[rows, re] = nr
            bu_ref[rows, im] = ni
            return nr, ni

        hr, hi = lax.fori_loop(0, tc, step, (h_ref[:, re], h_ref[:, im]), unroll=4)
        h_ref[:, re] = hr
        h_ref[:, im] = hi
    y = _dot(bu_ref[...].astype(BF16), cmat_ref[...]) + d_ref[...] * u.astype(F32)
    g = _gelu(y)
    z = _dot(g.astype(BF16), wglu_ref[...]) + bglu_ref[...]
    o_ref[...] = (g / (1.0 + jnp.exp(-z))).astype(o_ref.dtype)


def s5_mixer(u_tm, bmat, a_re, a_im, cmat, d, w_glu, b_glu, *, batch, seq, tc=64):
    W = u_tm.shape[1]
    ns = a_re.shape[1]
    tc = min(tc, seq)
    rows = tc * batch
    full = lambda a: pl.BlockSpec(a.shape, lambda t: (0,) * a.ndim)
    args = (bmat, a_re, a_im, cmat, d, w_glu, b_glu)
    return pl.pallas_call(
        functools.partial(_s5_kernel, batch=batch, tc=tc, ns=ns, lane_chunk=min(512, ns)),
        out_shape=jax.ShapeDtypeStruct(u_tm.shape, BF16), grid=(seq // tc,),
        in_specs=[pl.BlockSpec((rows, W), lambda t: (t, 0))] + [full(a) for a in args],
        out_specs=pl.BlockSpec((rows, W), lambda t: (t, 0)),
        scratch_shapes=[pltpu.VMEM((rows, 2 * ns), F32), pltpu.VMEM((batch, 2 * ns), F32)],
        compiler_params=_params("arbitrary"), name="s5_mixer",
    )(u_tm, *args)


def _top_sorted(s, k):
    rows = []
    for _ in range(k):
        mx = jnp.max(s, axis=0, keepdims=True)
        rows.append(mx)
        s = jnp.where(s >= mx, -jnp.inf, s)
    return jnp.concatenate(rows, axis=0)


def _sort_network(n):
    def merge(lo, hi, r):
        step = r * 2
        if step < hi - lo:
            yield from merge(lo, hi, step)
            yield from merge(lo + r, hi, step)
            yield from ((i, i + r) for i in range(lo + r, hi - r, step))
        else:
            yield (lo, lo + r)

    def sort(lo, hi):
        if hi - lo >= 1:
            mid = lo + (hi - lo) // 2
            yield from sort(lo, mid)
            yield from sort(mid + 1, hi)
            yield from merge(lo, hi, 1)

    return list(sort(0, n - 1))


def _top16_tiles(x):
    K = PEER_TOPK
    t = [x[i * SUBLANES:(i + 1) * SUBLANES, :] for i in range(K)]

    def cmpx(i, j):
        hi, lo = jnp.maximum(t[i], t[j]), jnp.minimum(t[i], t[j])
        t[i], t[j] = hi, lo

    for i, j in _sort_network(K):
        cmpx(i, j)
    for shift in (4, 2, 1):
        other = [pltpu.roll(v, shift, 0) for v in t]
        t = [jnp.maximum(t[i], other[K - 1 - i]) for i in range(K)]
        d = K // 2
        while d >= 1:
            for i in range(K):
                if i & d == 0:
                    cmpx(i, i + d)
            d //= 2
    return t


def _ranks_on_sublanes(tiles):
    sub = lax.broadcasted_iota(jnp.int32, (SUBLANES, LANES), 0)
    out = tiles[SUBLANES - 1]
    for r in range(SUBLANES - 2, -1, -1):
        out = jnp.where(sub == r, tiles[r], out)
    return out


def _peer_score_kernel(q_ref, keys_ref, s1_ref, s2_ref, a_ref, b_ref, tau_ref, *, tm, n_i):
    s1 = _dot_nt(keys_ref[0], q_ref[:, :PEER_HALF])
    s2 = _dot_nt(keys_ref[1], q_ref[:, PEER_HALF:])
    s2_ref[...] = s2
    for grp in range(PEER_N_KEYS // n_i):
        s1_ref[grp] = s1[grp * n_i:(grp + 1) * n_i, :]
    K = PEER_TOPK
    for c in range(tm // LANES):
        lanes = slice(c * LANES, (c + 1) * LANES)
        x1 = s1[:, lanes]
        x2 = s2[:, lanes]
        t1 = _top16_tiles(x1)
        t2 = _top16_tiles(x2)
        t2_lo, t2_hi = _ranks_on_sublanes(t2[:SUBLANES]), _ranks_on_sublanes(t2[SUBLANES:])
        cands = [t1[0] + t2_lo, t1[0] + t2_hi]
        cands += [t1[i] + t2_lo for i in range(1, SUBLANES)]
        cands += [_ranks_on_sublanes(t1[SUBLANES:]) + t2[0]]
        cand = jnp.concatenate(cands, axis=0)
        best = _top_sorted(cand, K)
        tau = best[K - 1:K]
        m = best[0:1]
        z = jnp.sum(jnp.where(cand >= tau, jnp.exp(cand - m), 0.0), axis=0, keepdims=True)
        tau_ref[:, lanes] = tau
        a = jnp.exp(x1 - t1[0][0:1]) / z
        for grp in range(PEER_N_KEYS // n_i):
            a_ref[grp, :, lanes] = a[grp * n_i:(grp + 1) * n_i, :]
        b_ref[:, lanes] = jnp.exp(x2 - t2[0][0:1])


def peer_scores(q, keys, *, n_i, tm=512):
    T = q.shape[0]
    tm = min(tm, T)
    H = PEER_HEADS
    n_grp = PEER_N_KEYS // n_i
    big = jax.ShapeDtypeStruct((H, PEER_N_KEYS, T), F32)
    big_spec = pl.BlockSpec((None, PEER_N_KEYS, tm), lambda i, h: (h, 0, i))
    grp = jax.ShapeDtypeStruct((H, n_grp, n_i, T), F32)
    grp_spec = pl.BlockSpec((None, n_grp, n_i, tm), lambda i, h: (h, 0, 0, i))
    return pl.pallas_call(
        functools.partial(_peer_score_kernel, tm=tm, n_i=n_i),
        out_shape=[grp, big, grp, big, jax.ShapeDtypeStruct((H, 1, T), F32)],
        grid=(T // tm, H),
        in_specs=[pl.BlockSpec((tm, 2 * PEER_HALF), lambda i, h: (i, h)),
                  pl.BlockSpec((2, PEER_N_KEYS, PEER_HALF), lambda i, h: (h, 0, 0))],
        out_specs=[grp_spec, big_spec, grp_spec, big_spec,
                   pl.BlockSpec((None, 1, tm), lambda i, h: (h, 0, i))],
        compiler_params=_params("parallel", "parallel"), name="peer_scores",
    )(q, keys)


def _gate_group(s1_ref, s2_ref, a_ref, b_ref, tau_ref, c, jb, n_i):
    heads = range(PEER_HEADS)
    lanes = slice(c * LANES, (c + 1) * LANES)
    rows = slice(jb * SUBLANES, (jb + 1) * SUBLANES)
    tau = [tau_ref[h, :, lanes] for h in heads]
    s2 = [s2_ref[h, rows, lanes] for h in heads]
    b = [b_ref[h, rows, lanes] for h in heads]
    out = []
    for ii in range(n_i):
        terms = [jnp.where(s1_ref[h, ii:ii + 1, lanes] + s2[h] >= tau[h],
                           a_ref[h, ii:ii + 1, lanes] * b[h], 0.0) for h in heads]
        while len(terms) > 1:
            terms = [x + y for x, y in zip(terms[0::2], terms[1::2])]
        out.append(terms[0])
    return out


def _exact_zero(tiles):
    while len(tiles) > 2:
        tiles = [x + y for x, y in zip(tiles[0::2], tiles[1::2])]
    if len(tiles) == 1:
        tiles = tiles * 2
    zero = [pltpu.bitcast(lax.shift_right_logical(pltpu.bitcast(t, jnp.uint32), jnp.uint32(32)), F32)
            for t in tiles]
    return jnp.concatenate(zero, axis=0).astype(BF16)


def _anchor(x, zeros):
    pack = 2 * SUBLANES
    rows_out = []
    for r in range(x.shape[0] // pack):
        row = x[r * pack:(r + 1) * pack, :]
        hits = sorted(((l0, z) for (rr, l0), z in zeros.items() if rr == r), key=lambda t: t[0])
        if hits:
            parts, pos = [], 0
            for l0, z in hits:
                if l0 > pos:
                    parts.append(row[:, pos:l0])
                parts.append(row[:, l0:l0 + LANES] + z)
                pos = l0 + LANES
            if pos < x.shape[1]:
                parts.append(row[:, pos:])
            row = jnp.concatenate(parts, axis=1)
        rows_out.append(row)
    return jnp.concatenate(rows_out, axis=0)


def _peer_mix_kernel(h_ref, u_ref, vt_ref, s1_ref, s1n_ref, s2_ref, a_ref, an_ref, b_ref, tau_ref, x_ref,
                     g_ref, o_ref, acc_ref, ht_ref, gs_ref, *, tm, te, final_norm):
    j = pl.program_id(1)
    n_i = te // PEER_N_KEYS
    n_c = tm // LANES
    n_jb = PEER_N_KEYS // SUBLANES
    c_lo = n_c // 2
    pack = 2 * SUBLANES

    def gates_lo(s1r, ar):
        out = []
        for c in range(c_lo):
            for jb in range(n_jb):
                tiles = _gate_group(s1r, s2_ref, ar, b_ref, tau_ref, c, jb, n_i)
                for ii, t in enumerate(tiles):
                    r0 = ii * PEER_N_KEYS + jb * SUBLANES
                    gs_ref[r0:r0 + SUBLANES, c * LANES:(c + 1) * LANES] = t
                out.append(tiles)
        return out

    @pl.when(j == 0)
    def _():
        acc_ref[...] = jnp.zeros_like(acc_ref)
        ht_ref[...] = h_ref[...].T
        gates_lo(s1_ref, a_ref)

    g_lo = gs_ref[...]
    u = u_ref[...]
    k_blocks = u.shape[1] // MXU_DEPTH
    n_r = te // pack
    groups = [(c, jb) for c in range(c_lo, n_c) for jb in range(n_jb)]
    g = [[[None] * n_c for _ in range(n_jb)] for _ in range(n_i)]
    zeros = {}
    for q, (c, jb) in enumerate(groups):
        tiles = _gate_group(s1_ref, s2_ref, a_ref, b_ref, tau_ref, c, jb, n_i)
        for ii, t in enumerate(tiles):
            g[ii][jb][c] = t
        r = (q * n_r) // len(groups)
        zeros[(r, ((r * k_blocks) // n_r) * MXU_DEPTH)] = _exact_zero(tiles)
    st = _dot(_anchor(u, zeros), ht_ref[...])
    g_hi = jnp.concatenate([jnp.concatenate(g[ii][jb][c_lo:], axis=1)
                            for ii in range(n_i) for jb in range(n_jb)], axis=0)
    wt = (jnp.concatenate([g_lo, g_hi], axis=1) * _gelu(st)).astype(BF16)

    vt = vt_ref[...]
    n_rv = vt.shape[0] // pack
    kb_v = te // MXU_DEPTH
    nxt = gates_lo(s1n_ref, an_ref)
    zeros = {}
    for q, tiles in enumerate(nxt):
        pos = (q * n_rv * kb_v) // len(nxt)
        zeros[(pos % n_rv, (pos // n_rv) * MXU_DEPTH)] = _exact_zero(tiles)
    acc_ref[...] += _dot(_anchor(vt, zeros), wt)

    @pl.when(j == pl.num_programs(1) - 1)
    def _():
        y = x_ref[...] + acc_ref[...].T
        if final_norm:
            ms = jnp.mean(y * y, axis=-1, keepdims=True)
            y = y * lax.rsqrt(ms + NORM_EPS) * g_ref[...]
        o_ref[...] = y


def peer_mix(h, u, vt, s1, s2, a, b, tau, x, g_final, *, final_norm, tm=512):
    T, D = x.shape
    E = u.shape[0]
    tm = min(tm, T)
    H = PEER_HEADS
    tok = pl.BlockSpec((tm, D), lambda i, j: (i, 0))
    n_i = s1.shape[2]
    te = n_i * PEER_N_KEYS
    ne = E // te
    sc = pl.BlockSpec((H, PEER_N_KEYS, tm), lambda i, j: (0, 0, i))
    grp = pl.BlockSpec((H, None, n_i, tm), lambda i, j: (0, j, 0, i))
    grp_next = pl.BlockSpec((H, None, n_i, tm), lambda i, j: (0, jnp.minimum(j + 1, ne - 1), 0, i))
    return pl.pallas_call(
        functools.partial(_peer_mix_kernel, tm=tm, te=te, final_norm=final_norm),
        out_shape=jax.ShapeDtypeStruct((T, D), F32), grid=(T // tm, ne),
        in_specs=[tok,
                  pl.BlockSpec((te, D), lambda i, j: (j, 0)),
                  pl.BlockSpec((None, D, te), lambda i, j: (j, 0, 0)),
                  grp, grp_next, sc, grp, grp_next, sc,
                  pl.BlockSpec((H, 1, tm), lambda i, j: (0, 0, i)),
                  tok,
                  pl.BlockSpec((1, D), lambda i, j: (0, 0))],
        out_specs=tok,
        scratch_shapes=[pltpu.VMEM((D, tm), F32), pltpu.VMEM((D, tm), BF16),
                        pltpu.VMEM((te, tm // 2), F32)],
        compiler_params=_params("arbitrary", "arbitrary"), name="peer_mix",
    )(h, u, vt, s1, s1, s2, a, a, b, tau, x, g_final.reshape(1, D).astype(F32))


def _rope_tables(positions):
    half = DIFF_HEAD_DIM // 2
    inv_freq = ROPE_THETA ** (-jnp.arange(half, dtype=F32) * 2.0 / DIFF_HEAD_DIM)
    ang = positions.astype(F32).reshape(-1, 1) * inv_freq
    cos, sin = jnp.cos(ang), jnp.sin(ang)
    return jnp.tile(cos, (1, 4)), jnp.concatenate([-sin, sin, -sin, sin], axis=1)


def _s5_params(a_re, a_im, log_step, b_re, b_im, c_re, c_im):
    G, N = a_re.shape
    P = b_re.shape[-1]
    step = jnp.exp(log_step.astype(F32))[:, None]
    mag = jnp.exp(a_re * step)
    ab_re = mag * jnp.cos(a_im * step)
    ab_im = mag * jnp.sin(a_im * step)
    den = a_re * a_re + a_im * a_im
    num_re = ab_re - 1.0
    f_re = (num_re * a_re + ab_im * a_im) / den
    f_im = (ab_im * a_re - num_re * a_im) / den
    bb_re = f_re[..., None] * b_re - f_im[..., None] * b_im
    bb_im = f_re[..., None] * b_im + f_im[..., None] * b_re
    eye = jnp.eye(G, dtype=F32)
    blk_in = lambda m: jnp.einsum('gnp,gh->gphn', m, eye).reshape(G * P, G * N)
    blk_out = lambda m: jnp.einsum('gpn,gh->gnhp', m, eye).reshape(G * N, G * P)
    bmat = jnp.concatenate([blk_in(bb_re), blk_in(bb_im)], axis=1).astype(BF16)
    cmat = jnp.concatenate([blk_out(c_re), -blk_out(c_im)], axis=0).astype(BF16)
    return bmat, ab_re.reshape(1, G * N), ab_im.reshape(1, G * N), cmat


def ab_block(x, i, layer, cos, sin, ab_norm, ab_w_in, ab_w_out, diff_lq1, diff_lk1, diff_lq2, diff_lk2,
             diff_subln, s5_a_re, s5_a_im, s5_log_step, s5_b_re, s5_b_im, s5_c_re, s5_c_im, s5_d,
             s5_w_glu, s5_b_glu, *, batch, seq):
    T = x.shape[0]
    diff_w = DIFF_HEADS * 2 * DIFF_HEAD_DIM
    s5_w = ab_w_in.shape[2] - 3 * diff_w
    lam_init = 0.8 - 0.6 * math.exp(-0.3 * layer)
    proj = norm_matmul(x, ab_norm[i], ab_w_in[i].astype(BF16), tm=TM_STREAMED, tn=TN_STREAMED, out_dtype=BF16,
                       rope=(0, 2 * diff_w // TN_STREAMED, (True,) * (TN_STREAMED // LANES)), cos=cos, sin=sin,
                       name="ab_in")
    a_out = diff_attention(proj, diff_lq1[i], diff_lk1[i], diff_lq2[i], diff_lk2[i], diff_subln[i],
                           batch=batch, seq=seq, lam_init=lam_init)
    bmat, a_re, a_im, cmat = _s5_params(s5_a_re[i], s5_a_im[i], s5_log_step[i], s5_b_re[i], s5_b_im[i],
                                        s5_c_re[i], s5_c_im[i])
    u_tm = proj[:, 3 * diff_w:].reshape(batch, seq, s5_w).transpose(1, 0, 2).reshape(T, s5_w)
    b_tm = s5_mixer(u_tm, bmat, a_re, a_im, cmat, s5_d[i].reshape(1, s5_w).astype(F32),
                    s5_w_glu[i].astype(BF16), s5_b_glu[i].reshape(1, s5_w).astype(F32),
                    batch=batch, seq=seq)
    b_out = b_tm.reshape(seq, batch, s5_w).transpose(1, 0, 2).reshape(T, s5_w)
    w_out = ab_w_out[i].astype(BF16)
    return matmul_residual([(a_out, w_out[:diff_w]), (b_out, w_out[diff_w:])], x, tm=TM_RESIDENT,
                           tn=x.shape[1], name="ab_out")


def mla_block(x, i, cos, sin, mla_norm, mla_w_in, mla_q_norm, mla_kv_norm, mla_w_uq, mla_w_ukv, mla_w_o,
              *, batch, seq):
    in_w = mla_w_in.shape[2]
    in_pad = (MLA_Q_RANK + MLA_KV_RANK) + LANES
    w_in = jnp.pad(mla_w_in[i], ((0, 0), (0, in_pad - in_w))).astype(BF16)
    proj = norm_matmul(x, mla_norm[i], w_in, tm=TM_RESIDENT, tn=in_pad, out_dtype=BF16,
                       rope=(0, 1, (False,) * (in_pad // LANES - 1) + (True,)), cos=cos, sin=sin, name="mla_in")
    w_uq = mla_w_uq[i].reshape(MLA_Q_RANK, MLA_HEADS, MLA_NOPE + MLA_ROPE)
    w_uq = jnp.pad(w_uq, ((0, 0), (0, 0), (0, MLA_QK_PAD - MLA_NOPE - MLA_ROPE)))
    w_uq = w_uq.reshape(MLA_Q_RANK, MLA_HEADS * MLA_QK_PAD).astype(BF16)
    q = norm_matmul(proj, mla_q_norm[i], w_uq, tm=TM_RESIDENT, tn=w_uq.shape[1], out_dtype=BF16, x_col_block=0,
                    rope=(0, 1, (False, True) * MLA_HEADS), cos=cos, sin=sin, name="mla_q")
    kv = norm_matmul(proj, mla_kv_norm[i], mla_w_ukv[i].astype(BF16), tm=TM_RESIDENT, tn=mla_w_ukv.shape[2],
                     out_dtype=BF16, x_col_block=MLA_Q_RANK // MLA_KV_RANK, name="mla_kv")
    o = mla_attention(q, kv, proj, batch=batch, seq=seq)
    return matmul_residual([(o, mla_w_o[i].astype(BF16))], x, tm=TM_RESIDENT, tn=x.shape[1], name="mla_out")


def xa_block(x, mem, layer, xa_norm, xa_mem_norm, xa_w_q, xa_w_kv, xa_w_o, *, batch, seq):
    mem_len = mem.shape[0] // batch
    q = norm_matmul(x, xa_norm[layer], xa_w_q[layer].astype(BF16), tm=TM_RESIDENT, tn=x.shape[1],
                    out_dtype=BF16, name="xa_q")
    kv = norm_matmul(mem, xa_mem_norm[layer], xa_w_kv[layer].astype(BF16), tm=TM_STREAMED, tn=TN_STREAMED,
                     out_dtype=BF16, name="xa_kv")
    o = cross_attention(q, kv, batch=batch, seq=seq, mem_len=mem_len)
    return matmul_residual([(o, xa_w_o[layer].astype(BF16))], x, tm=TM_RESIDENT, tn=x.shape[1], name="xa_out")


def peer_block(x, layer, ffn_norm, peer_w_query, peer_sub_keys, peer_u, peer_v, final_norm, *, last):
    pq, hn = norm_matmul(x, ffn_norm[layer], peer_w_query[layer].astype(BF16), tm=TM_RESIDENT,
                         tn=peer_w_query.shape[2], out_dtype=BF16, emit_xn=True, name="peer_q")
    keys = peer_sub_keys[layer].reshape(2 * PEER_HEADS, PEER_N_KEYS, PEER_HALF).astype(BF16)
    s1, s2, a, b, tau = peer_scores(pq, keys, n_i=PEER_TILE_KEYS)
    te = PEER_TILE_KEYS * PEER_N_KEYS
    vt = peer_v[layer].astype(BF16).reshape(-1, te, x.shape[1]).transpose(0, 2, 1)
    return peer_mix(hn, peer_u[layer].astype(BF16), vt, s1, s2, a, b, tau,
                    x, final_norm, final_norm=last)


def kernel(x, mem, positions, ab_norm, ab_w_in, ab_w_out, diff_lq1, diff_lk1, diff_lq2, diff_lk2, diff_subln, s5_a_re, s5_a_im, s5_log_step, s5_b_re, s5_b_im, s5_c_re, s5_c_im, s5_d, s5_w_glu, s5_b_glu, mla_norm, mla_w_in, mla_q_norm, mla_kv_norm, mla_w_uq, mla_w_ukv, mla_w_o, xa_norm, xa_mem_norm, xa_w_q, xa_w_kv, xa_w_o, ffn_norm, peer_w_query, peer_sub_keys, peer_u, peer_v, final_norm):
    batch, seq, D = x.shape
    depth = xa_norm.shape[0]
    x = x.reshape(batch * seq, D)
    mem = mem.reshape(-1, D)
    cos, sin = _rope_tables(positions)
    for layer in range(depth):
        i = layer // 2
        if layer % 2 == 0:
            x = ab_block(x, i, layer, cos, sin, ab_norm, ab_w_in, ab_w_out, diff_lq1, diff_lk1, diff_lq2,
                         diff_lk2, diff_subln, s5_a_re, s5_a_im, s5_log_step, s5_b_re, s5_b_im, s5_c_re,
                         s5_c_im, s5_d, s5_w_glu, s5_b_glu, batch=batch, seq=seq)
        else:
            x = mla_block(x, i, cos, sin, mla_norm, mla_w_in, mla_q_norm, mla_kv_norm, mla_w_uq, mla_w_ukv,
                          mla_w_o, batch=batch, seq=seq)
        x = xa_block(x, mem, layer, xa_norm, xa_mem_norm, xa_w_q, xa_w_kv, xa_w_o, batch=batch, seq=seq)
        x = peer_block(x, layer, ffn_norm, peer_w_query, peer_sub_keys, peer_u, peer_v, final_norm,
                       last=(layer == depth - 1))
    return x.reshape(batch, seq, D)
```

```python
import functools
import math

import jax
import jax.numpy as jnp
from jax import lax
from jax.experimental import pallas as pl
from jax.experimental.pallas import tpu as pltpu

F32 = jnp.float32
BF16 = jnp.bfloat16

NORM_EPS = 1e-6
ROPE_THETA = 10000.0
LANES = 128
SUBLANES = 8
MXU_DEPTH = 256
TM_RESIDENT = 512
TM_STREAMED = 1024
TN_STREAMED = 512
VMEM_LIMIT = 56 * 1024 * 1024

DIFF_HEADS = 12
DIFF_HEAD_DIM = 64
S5_GROUP = 16
S5_STATE = 64
MLA_HEADS = 16
MLA_Q_RANK = 768
MLA_KV_RANK = 256
MLA_NOPE = 128
MLA_ROPE = 64
MLA_QK_PAD = 256
XA_HEADS = 4
PEER_HEADS = 8
PEER_N_KEYS = 128
PEER_TOPK = 16
PEER_HALF = 128
PEER_TILE_KEYS = 4


def _dot(a, b):
    return jnp.dot(a, b, preferred_element_type=F32)


def _dot_nt(a, b):
    return lax.dot_general(a, b, (((1,), (1,)), ((), ())), preferred_element_type=F32)


def _gelu(x):
    return 0.5 * x * (1.0 + lax.erf(x * (2.0 ** -0.5)))


def _params(*sem):
    return pltpu.CompilerParams(dimension_semantics=sem, vmem_limit_bytes=VMEM_LIMIT)


def _rope_chunk(a, cos, sin, first_half):
    partner = jnp.where(first_half, pltpu.roll(a, LANES - 32, 1), pltpu.roll(a, 32, 1))
    return a * cos + partner * sin


def _norm_matmul_kernel(*refs, rope, emit_xn, tm, tn):
    x_ref, g_ref, w_ref, *rest = refs
    if rope is not None:
        cos_ref, sin_ref, *rest = rest
    o_ref, *rest = rest
    if emit_xn:
        xn_out_ref, *rest = rest
    (xn_ref,) = rest
    j = pl.program_id(1)

    @pl.when(j == 0)
    def _():
        x = x_ref[...].astype(F32)
        ms = jnp.mean(x * x, axis=-1, keepdims=True)
        xn = (x * lax.rsqrt(ms + NORM_EPS) * g_ref[...]).astype(BF16)
        xn_ref[...] = xn
        if emit_xn:
            xn_out_ref[...] = xn

    acc = _dot(xn_ref[...], w_ref[...])
    if rope is None:
        o_ref[...] = acc.astype(o_ref.dtype)
        return
    tile_lo, tile_hi, chunk_mask = rope
    in_range = jnp.logical_and(j >= tile_lo, j < tile_hi)

    @pl.when(in_range)
    def _():
        cos = cos_ref[...]
        sin = sin_ref[...]
        lane = lax.broadcasted_iota(jnp.int32, (tm, LANES), 1)
        first_half = (lane & 63) < 32
        for c in range(tn // LANES):
            a = acc[:, c * LANES:(c + 1) * LANES]
            if chunk_mask[c]:
                a = _rope_chunk(a, cos, sin, first_half)
            o_ref[:, c * LANES:(c + 1) * LANES] = a.astype(o_ref.dtype)

    @pl.when(jnp.logical_not(in_range))
    def _():
        o_ref[...] = acc.astype(o_ref.dtype)


def norm_matmul(x, g, w, *, tm, tn, out_dtype, x_col_block=0, rope=None, cos=None, sin=None,
                emit_xn=False, name=None):
    T = x.shape[0]
    K, N = w.shape
    tm = min(tm, T)
    assert T % tm == 0 and N % tn == 0 and tn % LANES == 0
    in_specs = [pl.BlockSpec((tm, K), lambda i, j: (i, x_col_block)),
                pl.BlockSpec((1, K), lambda i, j: (0, 0)),
                pl.BlockSpec((K, tn), lambda i, j: (0, j))]
    args = [x, g.reshape(1, K).astype(F32), w]
    if rope is not None:
        in_specs += [pl.BlockSpec((tm, LANES), lambda i, j: (i, 0))] * 2
        args += [cos, sin]
    out_shape = [jax.ShapeDtypeStruct((T, N), out_dtype)]
    out_specs = [pl.BlockSpec((tm, tn), lambda i, j: (i, j))]
    if emit_xn:
        out_shape.append(jax.ShapeDtypeStruct((T, K), BF16))
        out_specs.append(pl.BlockSpec((tm, K), lambda i, j: (i, 0)))
    res = pl.pallas_call(
        functools.partial(_norm_matmul_kernel, rope=rope, emit_xn=emit_xn, tm=tm, tn=tn),
        out_shape=out_shape, grid=(T // tm, N // tn), in_specs=in_specs, out_specs=out_specs,
        scratch_shapes=[pltpu.VMEM((tm, K), BF16)],
        compiler_params=_params("parallel", "arbitrary"), name=name)(*args)
    return res if emit_xn else res[0]


def _matmul_residual_kernel(*refs, n_pairs):
    a_refs = refs[:n_pairs]
    w_refs = refs[n_pairs:2 * n_pairs]
    res_ref, o_ref = refs[2 * n_pairs:]
    acc = res_ref[...]
    for a_ref, w_ref in zip(a_refs, w_refs):
        acc = acc + _dot(a_ref[...], w_ref[...])
    o_ref[...] = acc


def matmul_residual(pairs, res, *, tm, tn, name=None):
    T, N = res.shape
    tm = min(tm, T)
    assert T % tm == 0 and N % tn == 0
    in_specs = [pl.BlockSpec((tm, a.shape[1]), lambda i, j: (i, 0)) for a, _ in pairs]
    in_specs += [pl.BlockSpec((w.shape[0], tn), lambda i, j: (0, j)) for _, w in pairs]
    in_specs += [pl.BlockSpec((tm, tn), lambda i, j: (i, j))]
    return pl.pallas_call(
        functools.partial(_matmul_residual_kernel, n_pairs=len(pairs)),
        out_shape=jax.ShapeDtypeStruct((T, N), F32), grid=(T // tm, N // tn),
        in_specs=in_specs, out_specs=pl.BlockSpec((tm, tn), lambda i, j: (i, j)),
        compiler_params=_params("parallel", "arbitrary"), name=name,
    )(*[a for a, _ in pairs], *[w for _, w in pairs], res)


def _flash_q_block(qqs, load_k, load_v, qi, tq, dv):
    n = len(qqs)
    rows = qqs[0].shape[0]

    def kv_step(j, carry, masked):
        out = []
        for hh in range(n):
            m, l, acc = carry[hh]
            s = _dot_nt(qqs[hh], load_k(hh, j))
            if masked:
                row = lax.broadcasted_iota(jnp.int32, (rows, tq), 0)
                col = lax.broadcasted_iota(jnp.int32, (rows, tq), 1)
                s = jnp.where(col <= (row & (tq - 1)), s, -jnp.inf)
            m_new = jnp.maximum(m, jnp.max(s, axis=1, keepdims=True))
            alpha = jnp.exp(m - m_new)
            p = jnp.exp(s - m_new)
            l = alpha * l + jnp.sum(p, axis=1, keepdims=True)
            acc = alpha * acc + _dot(p.astype(BF16), load_v(hh, j))
            out.append((m_new, l, acc))
        return tuple(out)

    init = tuple((jnp.full((rows, 1), -jnp.inf, F32), jnp.zeros((rows, 1), F32),
                  jnp.zeros((rows, dv), F32)) for _ in range(n))
    carry = lax.fori_loop(0, qi, functools.partial(kv_step, masked=False), init)
    carry = kv_step(qi, carry, True)
    return [acc / l for _, l, acc in carry]


def _diff_attn_kernel(lq1_ref, lk1_ref, lq2_ref, lk2_ref, q_ref, k_ref, v_ref, g_ref, o_ref,
                      *, seq, tq, hp, lam_init):
    lam = (jnp.exp(jnp.sum(lq1_ref[...] * lk1_ref[...], keepdims=True))
           - jnp.exp(jnp.sum(lq2_ref[...] * lk2_ref[...], keepdims=True)) + lam_init)
    lane = lax.broadcasted_iota(jnp.int32, (tq, LANES), 1)
    scale = DIFF_HEAD_DIM ** -0.5
    head_lanes = lambda hh: slice(hh * LANES, (hh + 1) * LANES)

    def load_k(hh, j):
        return k_ref[pl.ds(pl.multiple_of(j * tq, tq), tq), head_lanes(hh)]

    def load_v(hh, j):
        return v_ref[pl.ds(pl.multiple_of(j * tq, tq), tq), head_lanes(hh)]

    def q_block(qi, carry):
        r0 = pl.multiple_of(qi * tq, tq)
        qqs = []
        for hh in range(hp):
            qs = q_ref[pl.ds(r0, tq), head_lanes(hh)].astype(F32) * scale
            q1 = jnp.where(lane < DIFF_HEAD_DIM, qs, 0.0).astype(BF16)
            q2 = jnp.where(lane >= DIFF_HEAD_DIM, qs, 0.0).astype(BF16)
            qqs.append(jnp.concatenate([q1, q2], axis=0))
        outs = _flash_q_block(qqs, load_k, load_v, qi, tq, LANES)
        for hh, o in enumerate(outs):
            o = o[:tq] - lam * o[tq:]
            ms = jnp.mean(o * o, axis=-1, keepdims=True)
            o = (o * lax.rsqrt(ms + NORM_EPS) * g_ref[...]) * (1.0 - lam_init)
            o_ref[pl.ds(r0, tq), head_lanes(hh)] = o.astype(o_ref.dtype)
        return carry

    lax.fori_loop(0, seq // tq, q_block, 0)


def diff_attention(qk, vu, lq1, lk1, lq2, lk2, subln, *, batch, seq, lam_init, tq=512, hp=2):
    T = qk.shape[0]
    G = DIFF_HEADS // hp
    W = hp * LANES
    tq = min(tq, seq)
    vec = lambda a: a.reshape(1, -1).astype(F32)
    small = pl.BlockSpec((1, DIFF_HEAD_DIM), lambda b, g: (0, 0))
    return pl.pallas_call(
        functools.partial(_diff_attn_kernel, seq=seq, tq=tq, hp=hp, lam_init=lam_init),
        out_shape=jax.ShapeDtypeStruct((T, DIFF_HEADS * LANES), BF16), grid=(batch, G),
        in_specs=[small, small, small, small,
                  pl.BlockSpec((seq, W), lambda b, g: (b, g)),
                  pl.BlockSpec((seq, W), lambda b, g: (b, G + g)),
                  pl.BlockSpec((seq, W), lambda b, g: (b, g)),
                  pl.BlockSpec((1, LANES), lambda b, g: (0, 0))],
        out_specs=pl.BlockSpec((seq, W), lambda b, g: (b, g)),
        compiler_params=_params("parallel", "parallel"), name="diff_attention",
    )(vec(lq1), vec(lk1), vec(lq2), vec(lk2), qk, qk, vu, vec(subln))


def _mla_attn_kernel(q_ref, kv_ref, kr_ref, o_ref, *, seq, tq, hp, scale):
    def load_k(hh, j):
        rows = pl.ds(pl.multiple_of(j * tq, tq), tq)
        kn = kv_ref[rows, hh * MLA_QK_PAD:hh * MLA_QK_PAD + MLA_NOPE]
        return jnp.concatenate([kn, kr_ref[rows, :]], axis=1)

    def load_v(hh, j):
        rows = pl.ds(pl.multiple_of(j * tq, tq), tq)
        return kv_ref[rows, hh * MLA_QK_PAD + MLA_NOPE:(hh + 1) * MLA_QK_PAD]

    def q_block(qi, carry):
        r0 = pl.multiple_of(qi * tq, tq)
        qqs = [(q_ref[pl.ds(r0, tq), hh * MLA_QK_PAD:(hh + 1) * MLA_QK_PAD].astype(F32) * scale).astype(BF16)
               for hh in range(hp)]
        outs = _flash_q_block(qqs, load_k, load_v, qi, tq, LANES)
        for hh, o in enumerate(outs):
            o_ref[pl.ds(r0, tq), hh * LANES:(hh + 1) * LANES] = o.astype(o_ref.dtype)
        return carry

    lax.fori_loop(0, seq // tq, q_block, 0)


def mla_attention(q, kv, proj, *, batch, seq, tq=512, hp=4):
    T = q.shape[0]
    tq = min(tq, seq)
    kr_block = (MLA_Q_RANK + MLA_KV_RANK) // LANES
    return pl.pallas_call(
        functools.partial(_mla_attn_kernel, seq=seq, tq=tq, hp=hp, scale=(MLA_NOPE + MLA_ROPE) ** -0.5),
        out_shape=jax.ShapeDtypeStruct((T, MLA_HEADS * LANES), BF16), grid=(batch, MLA_HEADS // hp),
        in_specs=[pl.BlockSpec((seq, hp * MLA_QK_PAD), lambda b, g: (b, g)),
                  pl.BlockSpec((seq, hp * MLA_QK_PAD), lambda b, g: (b, g)),
                  pl.BlockSpec((seq, LANES), lambda b, g: (b, kr_block))],
        out_specs=pl.BlockSpec((seq, hp * LANES), lambda b, g: (b, g)),
        compiler_params=_params("parallel", "parallel"), name="mla_attention",
    )(q, kv, proj)


def _xa_kernel(q_ref, k_ref, v_ref, o_ref, *, scale):
    q = (q_ref[...].astype(F32) * scale).astype(BF16)
    s = _dot_nt(q, k_ref[...])
    p = jnp.exp(s - jnp.max(s, axis=1, keepdims=True))
    l = jnp.sum(p, axis=1, keepdims=True)
    o_ref[...] = (_dot(p.astype(BF16), v_ref[...]) / l).astype(o_ref.dtype)


def cross_attention(q, kv, *, batch, seq, mem_len, tq=1024):
    T, D = q.shape
    hd = D // XA_HEADS
    tq = min(tq, seq)
    nq = seq // tq
    return pl.pallas_call(
        functools.partial(_xa_kernel, scale=hd ** -0.5),
        out_shape=jax.ShapeDtypeStruct((T, D), BF16), grid=(batch, XA_HEADS, nq),
        in_specs=[pl.BlockSpec((tq, hd), lambda b, h, i: (b * nq + i, h)),
                  pl.BlockSpec((mem_len, hd), lambda b, h, i: (b, h)),
                  pl.BlockSpec((mem_len, hd), lambda b, h, i: (b, XA_HEADS + h))],
        out_specs=pl.BlockSpec((tq, hd), lambda b, h, i: (b * nq + i, h)),
        compiler_params=_params("parallel", "parallel", "parallel"), name="cross_attention",
    )(q, kv, kv)


def _s5_kernel(u_ref, bmat_ref, are_ref, aim_ref, cmat_ref, d_ref, wglu_ref, bglu_ref, o_ref,
               bu_ref, h_ref, *, batch, tc, ns, lane_chunk):
    @pl.when(pl.program_id(0) == 0)
    def _():
        h_ref[...] = jnp.zeros_like(h_ref)

    u = u_ref[...]
    bu_ref[...] = _dot(u, bmat_ref[...])
    for lc in range(ns // lane_chunk):
        re = slice(lc * lane_chunk, (lc + 1) * lane_chunk)
        im = slice(ns + lc * lane_chunk, ns + (lc + 1) * lane_chunk)
        ar = jnp.broadcast_to(are_ref[:, re], (batch, lane_chunk))
        ai = jnp.broadcast_to(aim_ref[:, re], (batch, lane_chunk))

        def step(t, carry, re=re, im=im, ar=ar, ai=ai):
            hr, hi = carry
            rows = pl.ds(pl.multiple_of(t * batch, batch), batch)
            nr = ar * hr - ai * hi + bu_ref[rows, re]
            ni = ar * hi + ai * hr + bu_ref[rows, im]
            bu_ref[rows, re] = nr
            bu_ref[rows, im] = ni
            return nr, ni

        hr, hi = lax.fori_loop(0, tc, step, (h_ref[:, re], h_ref[:, im]), unroll=4)
        h_ref[:, re] = hr
        h_ref[:, im] = hi
    y = _dot(bu_ref[...].astype(BF16), cmat_ref[...]) + d_ref[...] * u.astype(F32)
    g = _gelu(y)
    z = _dot(g.astype(BF16), wglu_ref[...]) + bglu_ref[...]
    o_ref[...] = (g / (1.0 + jnp.exp(-z))).astype(o_ref.dtype)


def s5_mixer(u_tm, bmat, a_re, a_im, cmat, d, w_glu, b_glu, *, batch, seq, tc=64):
    W = u_tm.shape[1]
    ns = a_re.shape[1]
    tc = min(tc, seq)
    rows = tc * batch
    full = lambda a: pl.BlockSpec(a.shape, lambda t: (0,) * a.ndim)
    args = (bmat, a_re, a_im, cmat, d, w_glu, b_glu)
    return pl.pallas_call(
        functools.partial(_s5_kernel, batch=batch, tc=tc, ns=ns, lane_chunk=min(512, ns)),
        out_shape=jax.ShapeDtypeStruct(u_tm.shape, BF16), grid=(seq // tc,),
        in_specs=[pl.BlockSpec((rows, W), lambda t: (t, 0))] + [full(a) for a in args],
        out_specs=pl.BlockSpec((rows, W), lambda t: (t, 0)),
        scratch_shapes=[pltpu.VMEM((rows, 2 * ns), F32), pltpu.VMEM((batch, 2 * ns), F32)],
        compiler_params=_params("arbitrary"), name="s5_mixer",
    )(u_tm, *args)


def _top_sorted(s, k):
    rows = []
    for _ in range(k):
        mx = jnp.max(s, axis=0, keepdims=True)
        rows.append(mx)
        s = jnp.where(s >= mx, -jnp.inf, s)
    return jnp.concatenate(rows, axis=0)


def _sort_network(n):
    def merge(lo, hi, r):
        step = r * 2
        if step < hi - lo:
            yield from merge(lo, hi, step)
            yield from merge(lo + r, hi, step)
            yield from ((i, i + r) for i in range(lo + r, hi - r, step))
        else:
            yield (lo, lo + r)

    def sort(lo, hi):
        if hi - lo >= 1:
            mid = lo + (hi - lo) // 2
            yield from sort(lo, mid)
            yield from sort(mid + 1, hi)
            yield from merge(lo, hi, 1)

    return list(sort(0, n - 1))


def _top16_tiles(x):
    K = PEER_TOPK
    t = [x[i * SUBLANES:(i + 1) * SUBLANES, :] for i in range(K)]

    def cmpx(i, j):
        hi, lo = jnp.maximum(t[i], t[j]), jnp.minimum(t[i], t[j])
        t[i], t[j] = hi, lo

    for i, j in _sort_network(K):
        cmpx(i, j)
    for shift in (4, 2, 1):
        other = [pltpu.roll(v, shift, 0) for v in t]
        t = [jnp.maximum(t[i], other[K - 1 - i]) for i in range(K)]
        d = K // 2
        while d >= 1:
            for i in range(K):
                if i & d == 0:
                    cmpx(i, i + d)
            d //= 2
    return t


def _ranks_on_sublanes(tiles):
    sub = lax.broadcasted_iota(jnp.int32, (SUBLANES, LANES), 0)
    out = tiles[SUBLANES - 1]
    for r in range(SUBLANES - 2, -1, -1):
        out = jnp.where(sub == r, tiles[r], out)
    return out


def _peer_score_kernel(q_ref, keys_ref, s1_ref, s2_ref, a_ref, b_ref, tau_ref, *, tm, n_i):
    s1 = _dot_nt(keys_ref[0], q_ref[:, :PEER_HALF])
    s2 = _dot_nt(keys_ref[1], q_ref[:, PEER_HALF:])
    s2_ref[...] = s2
    for grp in range(PEER_N_KEYS // n_i):
        s1_ref[grp] = s1[grp * n_i:(grp + 1) * n_i, :]
    K = PEER_TOPK
    for c in range(tm // LANES):
        lanes = slice(c * LANES, (c + 1) * LANES)
        x1 = s1[:, lanes]
        x2 = s2[:, lanes]
        t1 = _top16_tiles(x1)
        t2 = _top16_tiles(x2)
        t2_lo, t2_hi = _ranks_on_sublanes(t2[:SUBLANES]), _ranks_on_sublanes(t2[SUBLANES:])
        cands = [t1[0] + t2_lo, t1[0] + t2_hi]
        cands += [t1[i] + t2_lo for i in range(1, SUBLANES)]
        cands += [_ranks_on_sublanes(t1[SUBLANES:]) + t2[0]]
        cand = jnp.concatenate(cands, axis=0)
        best = _top_sorted(cand, K)
        tau = best[K - 1:K]
        m = best[0:1]
        z = jnp.sum(jnp.where(cand >= tau, jnp.exp(cand - m), 0.0), axis=0, keepdims=True)
        tau_ref[:, lanes] = tau
        a = jnp.exp(x1 - t1[0][0:1]) / z
        for grp in range(PEER_N_KEYS // n_i):
            a_ref[grp, :, lanes] = a[grp * n_i:(grp + 1) * n_i, :]
        b_ref[:, lanes] = jnp.exp(x2 - t2[0][0:1])


def peer_scores(q, keys, *, n_i, tm=512):
    T = q.shape[0]
    tm = min(tm, T)
    H = PEER_HEADS
    n_grp = PEER_N_KEYS // n_i
    big = jax.ShapeDtypeStruct((H, PEER_N_KEYS, T), F32)
    big_spec = pl.BlockSpec((None, PEER_N_KEYS, tm), lambda i, h: (h, 0, i))
    grp = jax.ShapeDtypeStruct((H, n_grp, n_i, T), F32)
    grp_spec = pl.BlockSpec((None, n_grp, n_i, tm), lambda i, h: (h, 0, 0, i))
    return pl.pallas_call(
        functools.partial(_peer_score_kernel, tm=tm, n_i=n_i),
        out_shape=[grp, big, grp, big, jax.ShapeDtypeStruct((H, 1, T), F32)],
        grid=(T // tm, H),
        in_specs=[pl.BlockSpec((tm, 2 * PEER_HALF), lambda i, h: (i, h)),
                  pl.BlockSpec((2, PEER_N_KEYS, PEER_HALF), lambda i, h: (h, 0, 0))],
        out_specs=[grp_spec, big_spec, grp_spec, big_spec,
                   pl.BlockSpec((None, 1, tm), lambda i, h: (h, 0, i))],
        compiler_params=_params("parallel", "parallel"), name="peer_scores",
    )(q, keys)


def _gate_group(s1_ref, s2_ref, a_ref, b_ref, tau_ref, c, jb, n_i):
    heads = range(PEER_HEADS)
    lanes = slice(c * LANES, (c + 1) * LANES)
    rows = slice(jb * SUBLANES, (jb + 1) * SUBLANES)
    tau = [tau_ref[h, :, lanes] for h in heads]
    s2 = [s2_ref[h, rows, lanes] for h in heads]
    b = [b_ref[h, rows, lanes] for h in heads]
    out = []
    for ii in range(n_i):
        terms = [jnp.where(s1_ref[h, ii:ii + 1, lanes] + s2[h] >= tau[h],
                           a_ref[h, ii:ii + 1, lanes] * b[h], 0.0) for h in heads]
        while len(terms) > 1:
            terms = [x + y for x, y in zip(terms[0::2], terms[1::2])]
        out.append(terms[0])
    return out


def _exact_zero(tiles):
    while len(tiles) > 2:
        tiles = [x + y for x, y in zip(tiles[0::2], tiles[1::2])]
    if len(tiles) == 1:
        tiles = tiles * 2
    zero = [pltpu.bitcast(lax.shift_right_logical(pltpu.bitcast(t, jnp.uint32), jnp.uint32(32)), F32)
            for t in tiles]
    return jnp.concatenate(zero, axis=0).astype(BF16)


def _anchor(x, zeros):
    pack = 2 * SUBLANES
    rows_out = []
    for r in range(x.shape[0] // pack):
        row = x[r * pack:(r + 1) * pack, :]
        hits = sorted(((l0, z) for (rr, l0), z in zeros.items() if rr == r), key=lambda t: t[0])
        if hits:
            parts, pos = [], 0
            for l0, z in hits:
                if l0 > pos:
                    parts.append(row[:, pos:l0])
                parts.append(row[:, l0:l0 + LANES] + z)
                pos = l0 + LANES
            if pos < x.shape[1]:
                parts.append(row[:, pos:])
            row = jnp.concatenate(parts, axis=1)
        rows_out.append(row)
    return jnp.concatenate(rows_out, axis=0)


def _peer_mix_kernel(h_ref, u_ref, vt_ref, s1_ref, s1n_ref, s2_ref, a_ref, an_ref, b_ref, tau_ref, x_ref,
                     g_ref, o_ref, acc_ref, ht_ref, gs_ref, *, tm, te, final_norm):
    j = pl.program_id(1)
    n_i = te // PEER_N_KEYS
    n_c = tm // LANES
    n_jb = PEER_N_KEYS // SUBLANES
    c_lo = n_c // 2
    pack = 2 * SUBLANES

    def gates_lo(s1r, ar):
        out = []
        for c in range(c_lo):
            for jb in range(n_jb):
                tiles = _gate_group(s1r, s2_ref, ar, b_ref, tau_ref, c, jb, n_i)
                for ii, t in enumerate(tiles):
                    r0 = ii * PEER_N_KEYS + jb * SUBLANES
                    gs_ref[r0:r0 + SUBLANES, c * LANES:(c + 1) * LANES] = t
                out.append(tiles)
        return out

    @pl.when(j == 0)
    def _():
        acc_ref[...] = jnp.zeros_like(acc_ref)
        ht_ref[...] = h_ref[...].T
        gates_lo(s1_ref, a_ref)

    g_lo = gs_ref[...]
    u = u_ref[...]
    k_blocks = u.shape[1] // MXU_DEPTH
    n_r = te // pack
    groups = [(c, jb) for c in range(c_lo, n_c) for jb in range(n_jb)]
    g = [[[None] * n_c for _ in range(n_jb)] for _ in range(n_i)]
    zeros = {}
    for q, (c, jb) in enumerate(groups):
        tiles = _gate_group(s1_ref, s2_ref, a_ref, b_ref, tau_ref, c, jb, n_i)
        for ii, t in enumerate(tiles):
            g[ii][jb][c] = t
        r = (q * n_r) // len(groups)
        zeros[(r, ((r * k_blocks) // n_r) * MXU_DEPTH)] = _exact_zero(tiles)
    st = _dot(_anchor(u, zeros), ht_ref[...])
    g_hi = jnp.concatenate([jnp.concatenate(g[ii][jb][c_lo:], axis=1)
                            for ii in range(n_i) for jb in range(n_jb)], axis=0)
    wt = (jnp.concatenate([g_lo, g_hi], axis=1) * _gelu(st)).astype(BF16)

    vt = vt_ref[...]
    n_rv = vt.shape[0] // pack
    kb_v = te // MXU_DEPTH
    nxt = gates_lo(s1n_ref, an_ref)
    zeros = {}
    for q, tiles in enumerate(nxt):
        pos = (q * n_rv * kb_v) // len(nxt)
        zeros[(pos % n_rv, (pos // n_rv) * MXU_DEPTH)] = _exact_zero(tiles)
    acc_ref[...] += _dot(_anchor(vt, zeros), wt)

    @pl.when(j == pl.num_programs(1) - 1)
    def _():
        y = x_ref[...] + acc_ref[...].T
        if final_norm:
            ms = jnp.mean(y * y, axis=-1, keepdims=True)
            y = y * lax.rsqrt(ms + NORM_EPS) * g_ref[...]
        o_ref[...] = y


def peer_mix(h, u, vt, s1, s2, a, b, tau, x, g_final, *, final_norm, tm=512):
    T, D = x.shape
    E = u.shape[0]
    tm = min(tm, T)
    H = PEER_HEADS
    tok = pl.BlockSpec((tm, D), lambda i, j: (i, 0))
    n_i = s1.shape[2]
    te = n_i * PEER_N_KEYS
    ne = E // te
    sc = pl.BlockSpec((H, PEER_N_KEYS, tm), lambda i, j: (0, 0, i))
    grp = pl.BlockSpec((H, None, n_i, tm), lambda i, j: (0, j, 0, i))
    grp_next = pl.BlockSpec((H, None, n_i, tm), lambda i, j: (0, jnp.minimum(j + 1, ne - 1), 0, i))
    return pl.pallas_call(
        functools.partial(_peer_mix_kernel, tm=tm, te=te, final_norm=final_norm),
        out_shape=jax.ShapeDtypeStruct((T, D), F32), grid=(T // tm, ne),
        in_specs=[tok,
                  pl.BlockSpec((te, D), lambda i, j: (j, 0)),
                  pl.BlockSpec((None, D, te), lambda i, j: (j, 0, 0)),
                  grp, grp_next, sc, grp, grp_next, sc,
                  pl.BlockSpec((H, 1, tm), lambda i, j: (0, 0, i)),
                  tok,
                  pl.BlockSpec((1, D), lambda i, j: (0, 0))],
        out_specs=tok,
        scratch_shapes=[pltpu.VMEM((D, tm), F32), pltpu.VMEM((D, tm), BF16),
                        pltpu.VMEM((te, tm // 2), F32)],
        compiler_params=_params("arbitrary", "arbitrary"), name="peer_mix",
    )(h, u, vt, s1, s1, s2, a, a, b, tau, x, g_final.reshape(1, D).astype(F32))


def _rope_tables(positions):
    half = DIFF_HEAD_DIM // 2
    inv_freq = ROPE_THETA ** (-jnp.arange(half, dtype=F32) * 2.0 / DIFF_HEAD_DIM)
    ang = positions.astype(F32).reshape(-1, 1) * inv_freq
    cos, sin = jnp.cos(ang), jnp.sin(ang)
    return jnp.tile(cos, (1, 4)), jnp.concatenate([-sin, sin, -sin, sin], axis=1)


def _s5_params(a_re, a_im, log_step, b_re, b_im, c_re, c_im):
    G, N = a_re.shape
    P = b_re.shape[-1]
    step = jnp.exp(log_step.astype(F32))[:, None]
    mag = jnp.exp(a_re * step)
    ab_re = mag * jnp.cos(a_im * step)
    ab_im = mag * jnp.sin(a_im * step)
    den = a_re * a_re + a_im * a_im
    num_re = ab_re - 1.0
    f_re = (num_re * a_re + ab_im * a_im) / den
    f_im = (ab_im * a_re - num_re * a_im) / den
    bb_re = f_re[..., None] * b_re - f_im[..., None] * b_im
    bb_im = f_re[..., None] * b_im + f_im[..., None] * b_re
    eye = jnp.eye(G, dtype=F32)
    blk_in = lambda m: jnp.einsum('gnp,gh->gphn', m, eye).reshape(G * P, G * N)
    blk_out = lambda m: jnp.einsum('gpn,gh->gnhp', m, eye).reshape(G * N, G * P)
    bmat = jnp.concatenate([blk_in(bb_re), blk_in(bb_im)], axis=1).astype(BF16)
    cmat = jnp.concatenate([blk_out(c_re), -blk_out(c_im)], axis=0).astype(BF16)
    return bmat, ab_re.reshape(1, G * N), ab_im.reshape(1, G * N), cmat


def ab_block(x, i, layer, cos, sin, ab_norm, ab_w_in, ab_w_out, diff_lq1, diff_lk1, diff_lq2, diff_lk2,
             diff_subln, s5_a_re, s5_a_im, s5_log_step, s5_b_re, s5_b_im, s5_c_re, s5_c_im, s5_d,
             s5_w_glu, s5_b_glu, *, batch, seq):
    T = x.shape[0]
    diff_w = DIFF_HEADS * 2 * DIFF_HEAD_DIM
    s5_w = ab_w_in.shape[2] - 3 * diff_w
    lam_init = 0.8 - 0.6 * math.exp(-0.3 * layer)
    w_in = ab_w_in[i].astype(BF16)
    qk = norm_matmul(x, ab_norm[i], w_in[:, :2 * diff_w], tm=TM_RESIDENT, tn=2 * diff_w, out_dtype=BF16,
                     rope=(0, 1, (True,) * (2 * diff_w // LANES)), cos=cos, sin=sin, name="ab_in_qk")
    vu = norm_matmul(x, ab_norm[i], w_in[:, 2 * diff_w:], tm=TM_RESIDENT, tn=diff_w + s5_w, out_dtype=BF16,
                     name="ab_in_vu")
    a_out = diff_attention(qk, vu, diff_lq1[i], diff_lk1[i], diff_lq2[i], diff_lk2[i], diff_subln[i],
                           batch=batch, seq=seq, lam_init=lam_init)
    bmat, a_re, a_im, cmat = _s5_params(s5_a_re[i], s5_a_im[i], s5_log_step[i], s5_b_re[i], s5_b_im[i],
                                        s5_c_re[i], s5_c_im[i])
    u_tm = vu[:, diff_w:].reshape(batch, seq, s5_w).transpose(1, 0, 2).reshape(T, s5_w)
    b_tm = s5_mixer(u_tm, bmat, a_re, a_im, cmat, s5_d[i].reshape(1, s5_w).astype(F32),
                    s5_w_glu[i].astype(BF16), s5_b_glu[i].reshape(1, s5_w).astype(F32),
                    batch=batch, seq=seq)
    b_out = b_tm.reshape(seq, batch, s5_w).transpose(1, 0, 2).reshape(T, s5_w)
    w_out = ab_w_out[i].astype(BF16)
    return matmul_residual([(a_out, w_out[:diff_w]), (b_out, w_out[diff_w:])], x, tm=TM_RESIDENT,
                           tn=x.shape[1], name="ab_out")


def mla_block(x, i, cos, sin, mla_norm, mla_w_in, mla_q_norm, mla_kv_norm, mla_w_uq, mla_w_ukv, mla_w_o,
              *, batch, seq):
    in_w = mla_w_in.shape[2]
    in_pad = (MLA_Q_RANK + MLA_KV_RANK) + LANES
    w_in = jnp.pad(mla_w_in[i], ((0, 0), (0, in_pad - in_w))).astype(BF16)
    proj = norm_matmul(x, mla_norm[i], w_in, tm=TM_RESIDENT, tn=in_pad, out_dtype=BF16,
                       rope=(0, 1, (False,) * (in_pad // LANES - 1) + (True,)), cos=cos, sin=sin, name="mla_in")
    w_uq = mla_w_uq[i].reshape(MLA_Q_RANK, MLA_HEADS, MLA_NOPE + MLA_ROPE)
    w_uq = jnp.pad(w_uq, ((0, 0), (0, 0), (0, MLA_QK_PAD - MLA_NOPE - MLA_ROPE)))
    w_uq = w_uq.reshape(MLA_Q_RANK, MLA_HEADS * MLA_QK_PAD).astype(BF16)
    q = norm_matmul(proj, mla_q_norm[i], w_uq, tm=TM_RESIDENT, tn=w_uq.shape[1], out_dtype=BF16, x_col_block=0,
                    rope=(0, 1, (False, True) * MLA_HEADS), cos=cos, sin=sin, name="mla_q")
    kv = norm_matmul(proj, mla_kv_norm[i], mla_w_ukv[i].astype(BF16), tm=TM_RESIDENT, tn=mla_w_ukv.shape[2],
                     out_dtype=BF16, x_col_block=MLA_Q_RANK // MLA_KV_RANK, name="mla_kv")
    o = mla_attention(q, kv, proj, batch=batch, seq=seq)
    return matmul_residual([(o, mla_w_o[i].astype(BF16))], x, tm=TM_RESIDENT, tn=x.shape[1], name="mla_out")


def xa_block(x, mem, layer, xa_norm, xa_mem_norm, xa_w_q, xa_w_kv, xa_w_o, *, batch, seq):
    mem_len = mem.shape[0] // batch
    q = norm_matmul(x, xa_norm[layer], xa_w_q[layer].astype(BF16), tm=TM_RESIDENT, tn=x.shape[1],
                    out_dtype=BF16, name="xa_q")
    kv = norm_matmul(mem, xa_mem_norm[layer], xa_w_kv[layer].astype(BF16), tm=TM_STREAMED, tn=TN_STREAMED,
                     out_dtype=BF16, name="xa_kv")
    o = cross_attention(q, kv, batch=batch, seq=seq, mem_len=mem_len)
    return matmul_residual([(o, xa_w_o[layer].astype(BF16))], x, tm=TM_RESIDENT, tn=x.shape[1], name="xa_out")


def peer_block(x, layer, ffn_norm, peer_w_query, peer_sub_keys, peer_u, peer_v, final_norm, *, last):
    pq, hn = norm_matmul(x, ffn_norm[layer], peer_w_query[layer].astype(BF16), tm=TM_RESIDENT,
                         tn=peer_w_query.shape[2], out_dtype=BF16, emit_xn=True, name="peer_q")
    keys = peer_sub_keys[layer].reshape(2 * PEER_HEADS, PEER_N_KEYS, PEER_HALF).astype(BF16)
    s1, s2, a, b, tau = peer_scores(pq, keys, n_i=PEER_TILE_KEYS)
    te = PEER_TILE_KEYS * PEER_N_KEYS
    vt = peer_v[layer].astype(BF16).reshape(-1, te, x.shape[1]).transpose(0, 2, 1)
    return peer_mix(hn, peer_u[layer].astype(BF16), vt, s1, s2, a, b, tau,
                    x, final_norm, final_norm=last)


def kernel(x, mem, positions, ab_norm, ab_w_in, ab_w_out, diff_lq1, diff_lk1, diff_lq2, diff_lk2, diff_subln, s5_a_re, s5_a_im, s5_log_step, s5_b_re, s5_b_im, s5_c_re, s5_c_im, s5_d, s5_w_glu, s5_b_glu, mla_norm, mla_w_in, mla_q_norm, mla_kv_norm, mla_w_uq, mla_w_ukv, mla_w_o, xa_norm, xa_mem_norm, xa_w_q, xa_w_kv, xa_w_o, ffn_norm, peer_w_query, peer_sub_keys, peer_u, peer_v, final_norm):
    batch, seq, D = x.shape
    depth = xa_norm.shape[0]
    x = x.reshape(batch * seq, D)
    mem = mem.reshape(-1, D)
    cos, sin = _rope_tables(positions)
    for layer in range(depth):
        i = layer // 2
        if layer % 2 == 0:
            x = ab_block(x, i, layer, cos, sin, ab_norm, ab_w_in, ab_w_out, diff_lq1, diff_lk1, diff_lq2,
                         diff_lk2, diff_subln, s5_a_re, s5_a_im, s5_log_step, s5_b_re, s5_b_im, s5_c_re,
                         s5_c_im, s5_d, s5_w_glu, s5_b_glu, batch=batch, seq=seq)
        else:
            x = mla_block(x, i, cos, sin, mla_norm, mla_w_in, mla_q_norm, mla_kv_norm, mla_w_uq, mla_w_ukv,
                          mla_w_o, batch=batch, seq=seq)
        x = xa_block(x, mem, layer, xa_norm, xa_mem_norm, xa_w_q, xa_w_kv, xa_w_o, batch=batch, seq=seq)
        x = peer_block(x, layer, ffn_norm, peer_w_query, peer_sub_keys, peer_u, peer_v, final_norm,
                       last=(layer == depth - 1))
    return x.reshape(batch, seq, D)
```

```python
import functools
import math

import jax
import jax.numpy as jnp
from jax import lax
from jax.experimental import pallas as pl
from jax.experimental.pallas import tpu as pltpu

F32 = jnp.float32
BF16 = jnp.bfloat16

NORM_EPS = 1e-6
ROPE_THETA = 10000.0
LANES = 128
SUBLANES = 8
MXU_DEPTH = 256
TM_RESIDENT = 512
TM_STREAMED = 1024
TN_STREAMED = 512
VMEM_LIMIT = 56 * 1024 * 1024

DIFF_HEADS = 12
DIFF_HEAD_DIM = 64
S5_GROUP = 16
S5_STATE = 64
MLA_HEADS = 16
MLA_Q_RANK = 768
MLA_KV_RANK = 256
MLA_NOPE = 128
MLA_ROPE = 64
MLA_QK_PAD = 256
XA_HEADS = 4
PEER_HEADS = 8
PEER_N_KEYS = 128
PEER_TOPK = 16
PEER_HALF = 128
PEER_TILE_KEYS = 4


def _dot(a, b):
    return jnp.dot(a, b, preferred_element_type=F32)


def _dot_nt(a, b):
    return lax.dot_general(a, b, (((1,), (1,)), ((), ())), preferred_element_type=F32)


def _gelu(x):
    return 0.5 * x * (1.0 + lax.erf(x * (2.0 ** -0.5)))


def _params(*sem):
    return pltpu.CompilerParams(dimension_semantics=sem, vmem_limit_bytes=VMEM_LIMIT)


def _rope_chunk(a, cos, sin, first_half):
    partner = jnp.where(first_half, pltpu.roll(a, LANES - 32, 1), pltpu.roll(a, 32, 1))
    return a * cos + partner * sin


def _norm_matmul_kernel(*refs, rope, emit_xn, tm, tn):
    x_ref, g_ref, w_ref, *rest = refs
    if rope is not None:
        cos_ref, sin_ref, *rest = rest
    o_ref, *rest = rest
    if emit_xn:
        xn_out_ref, *rest = rest
    (xn_ref,) = rest
    j = pl.program_id(1)

    @pl.when(j == 0)
    def _():
        x = x_ref[...].astype(F32)
        ms = jnp.mean(x * x, axis=-1, keepdims=True)
        xn = (x * lax.rsqrt(ms + NORM_EPS) * g_ref[...]).astype(BF16)
        xn_ref[...] = xn
        if emit_xn:
            xn_out_ref[...] = xn

    acc = _dot(xn_ref[...], w_ref[...])
    if rope is None:
        o_ref[...] = acc.astype(o_ref.dtype)
        return
    tile_lo, tile_hi, chunk_mask = rope
    in_range = jnp.logical_and(j >= tile_lo, j < tile_hi)

    @pl.when(in_range)
    def _():
        cos = cos_ref[...]
        sin = sin_ref[...]
        lane = lax.broadcasted_iota(jnp.int32, (tm, LANES), 1)
        first_half = (lane & 63) < 32
        for c in range(tn // LANES):
            a = acc[:, c * LANES:(c + 1) * LANES]
            if chunk_mask[c]:
                a = _rope_chunk(a, cos, sin, first_half)
            o_ref[:, c * LANES:(c + 1) * LANES] = a.astype(o_ref.dtype)

    @pl.when(jnp.logical_not(in_range))
    def _():
        o_ref[...] = acc.astype(o_ref.dtype)


def norm_matmul(x, g, w, *, tm, tn, out_dtype, x_col_block=0, rope=None, cos=None, sin=None,
                emit_xn=False, name=None):
    T = x.shape[0]
    K, N = w.shape
    tm = min(tm, T)
    assert T % tm == 0 and N % tn == 0 and tn % LANES == 0
    in_specs = [pl.BlockSpec((tm, K), lambda i, j: (i, x_col_block)),
                pl.BlockSpec((1, K), lambda i, j: (0, 0)),
                pl.BlockSpec((K, tn), lambda i, j: (0, j))]
    args = [x, g.reshape(1, K).astype(F32), w]
    if rope is not None:
        in_specs += [pl.BlockSpec((tm, LANES), lambda i, j: (i, 0))] * 2
        args += [cos, sin]
    out_shape = [jax.ShapeDtypeStruct((T, N), out_dtype)]
    out_specs = [pl.BlockSpec((tm, tn), lambda i, j: (i, j))]
    if emit_xn:
        out_shape.append(jax.ShapeDtypeStruct((T, K), BF16))
        out_specs.append(pl.BlockSpec((tm, K), lambda i, j: (i, 0)))
    res = pl.pallas_call(
        functools.partial(_norm_matmul_kernel, rope=rope, emit_xn=emit_xn, tm=tm, tn=tn),
        out_shape=out_shape, grid=(T // tm, N // tn), in_specs=in_specs, out_specs=out_specs,
        scratch_shapes=[pltpu.VMEM((tm, K), BF16)],
        compiler_params=_params("parallel", "arbitrary"), name=name)(*args)
    return res if emit_xn else res[0]


def _matmul_residual_kernel(*refs, n_pairs):
    a_refs = refs[:n_pairs]
    w_refs = refs[n_pairs:2 * n_pairs]
    res_ref, o_ref = refs[2 * n_pairs:]
    acc = res_ref[...]
    for a_ref, w_ref in zip(a_refs, w_refs):
        acc = acc + _dot(a_ref[...], w_ref[...])
    o_ref[...] = acc


def matmul_residual(pairs, res, *, tm, tn, name=None):
    T, N = res.shape
    tm = min(tm, T)
    assert T % tm == 0 and N % tn == 0
    in_specs = [pl.BlockSpec((tm, a.shape[1]), lambda i, j: (i, 0)) for a, _ in pairs]
    in_specs += [pl.BlockSpec((w.shape[0], tn), lambda i, j: (0, j)) for _, w in pairs]
    in_specs += [pl.BlockSpec((tm, tn), lambda i, j: (i, j))]
    return pl.pallas_call(
        functools.partial(_matmul_residual_kernel, n_pairs=len(pairs)),
        out_shape=jax.ShapeDtypeStruct((T, N), F32), grid=(T // tm, N // tn),
        in_specs=in_specs, out_specs=pl.BlockSpec((tm, tn), lambda i, j: (i, j)),
        compiler_params=_params("parallel", "arbitrary"), name=name,
    )(*[a for a, _ in pairs], *[w for _, w in pairs], res)


def _flash_q_block(qqs, load_k, load_v, qi, tq, dv):
    n = len(qqs)
    rows = qqs[0].shape[0]

    def kv_step(j, carry, masked):
        out = []
        for hh in range(n):
            m, l, acc = carry[hh]
            s = _dot_nt(qqs[hh], load_k(hh, j))
            if masked:
                row = lax.broadcasted_iota(jnp.int32, (rows, tq), 0)
                col = lax.broadcasted_iota(jnp.int32, (rows, tq), 1)
                s = jnp.where(col <= (row & (tq - 1)), s, -jnp.inf)
            m_new = jnp.maximum(m, jnp.max(s, axis=1, keepdims=True))
            alpha = jnp.exp(m - m_new)
            p = jnp.exp(s - m_new)
            l = alpha * l + jnp.sum(p, axis=1, keepdims=True)
            acc = alpha * acc + _dot(p.astype(BF16), load_v(hh, j))
            out.append((m_new, l, acc))
        return tuple(out)

    init = tuple((jnp.full((rows, 1), -jnp.inf, F32), jnp.zeros((rows, 1), F32),
                  jnp.zeros((rows, dv), F32)) for _ in range(n))
    carry = lax.fori_loop(0, qi, functools.partial(kv_step, masked=False), init)
    carry = kv_step(qi, carry, True)
    return [acc / l for _, l, acc in carry]


def _diff_attn_kernel(lq1_ref, lk1_ref, lq2_ref, lk2_ref, q_ref, k_ref, v_ref, g_ref, o_ref,
                      *, seq, tq, hp, lam_init):
    lam = (jnp.exp(jnp.sum(lq1_ref[...] * lk1_ref[...], keepdims=True))
           - jnp.exp(jnp.sum(lq2_ref[...] * lk2_ref[...], keepdims=True)) + lam_init)
    lane = lax.broadcasted_iota(jnp.int32, (tq, LANES), 1)
    scale = DIFF_HEAD_DIM ** -0.5
    head_lanes = lambda hh: slice(hh * LANES, (hh + 1) * LANES)

    def load_k(hh, j):
        return k_ref[pl.ds(pl.multiple_of(j * tq, tq), tq), head_lanes(hh)]

    def load_v(hh, j):
        return v_ref[pl.ds(pl.multiple_of(j * tq, tq), tq), head_lanes(hh)]

    def q_block(qi, carry):
        r0 = pl.multiple_of(qi * tq, tq)
        qqs = []
        for hh in range(hp):
            qs = q_ref[pl.ds(r0, tq), head_lanes(hh)].astype(F32) * scale
            q1 = jnp.where(lane < DIFF_HEAD_DIM, qs, 0.0).astype(BF16)
            q2 = jnp.where(lane >= DIFF_HEAD_DIM, qs, 0.0).astype(BF16)
            qqs.append(jnp.concatenate([q1, q2], axis=0))
        outs = _flash_q_block(qqs, load_k, load_v, qi, tq, LANES)
        for hh, o in enumerate(outs):
            o = o[:tq] - lam * o[tq:]
            ms = jnp.mean(o * o, axis=-1, keepdims=True)
            o = (o * lax.rsqrt(ms + NORM_EPS) * g_ref[...]) * (1.0 - lam_init)
            o_ref[pl.ds(r0, tq), head_lanes(hh)] = o.astype(o_ref.dtype)
        return carry

    lax.fori_loop(0, seq // tq, q_block, 0)


def diff_attention(qk, vu, lq1, lk1, lq2, lk2, subln, *, batch, seq, lam_init, tq=512, hp=2):
    T = qk.shape[0]
    G = DIFF_HEADS // hp
    W = hp * LANES
    tq = min(tq, seq)
    vec = lambda a: a.reshape(1, -1).astype(F32)
    small = pl.BlockSpec((1, DIFF_HEAD_DIM), lambda b, g: (0, 0))
    return pl.pallas_call(
        functools.partial(_diff_attn_kernel, seq=seq, tq=tq, hp=hp, lam_init=lam_init),
        out_shape=jax.ShapeDtypeStruct((T, DIFF_HEADS * LANES), BF16), grid=(batch, G),
        in_specs=[small, small, small, small,
                  pl.BlockSpec((seq, W), lambda b, g: (b, g)),
                  pl.BlockSpec((seq, W), lambda b, g: (b, G + g)),
                  pl.BlockSpec((seq, W), lambda b, g: (b, g)),
                  pl.BlockSpec((1, LANES), lambda b, g: (0, 0))],
        out_specs=pl.BlockSpec((seq, W), lambda b, g: (b, g)),
        compiler_params=_params("parallel", "parallel"), name="diff_attention",
    )(vec(lq1), vec(lk1), vec(lq2), vec(lk2), qk, qk, vu, vec(subln))


def _mla_attn_kernel(q_ref, kv_ref, kr_ref, o_ref, *, seq, tq, hp, scale):
    def load_k(hh, j):
        rows = pl.ds(pl.multiple_of(j * tq, tq), tq)
        kn = kv_ref[rows, hh * MLA_QK_PAD:hh * MLA_QK_PAD + MLA_NOPE]
        return jnp.concatenate([kn, kr_ref[rows, :]], axis=1)

    def load_v(hh, j):
        rows = pl.ds(pl.multiple_of(j * tq, tq), tq)
        return kv_ref[rows, hh * MLA_QK_PAD + MLA_NOPE:(hh + 1) * MLA_QK_PAD]

    def q_block(qi, carry):
        r0 = pl.multiple_of(qi * tq, tq)
        qqs = [(q_ref[pl.ds(r0, tq), hh * MLA_QK_PAD:(hh + 1) * MLA_QK_PAD].astype(F32) * scale).astype(BF16)
               for hh in range(hp)]
        outs = _flash_q_block(qqs, load_k, load_v, qi, tq, LANES)
        for hh, o in enumerate(outs):
            o_ref[pl.ds(r0, tq), hh * LANES:(hh + 1) * LANES] = o.astype(o_ref.dtype)
        return carry

    lax.fori_loop(0, seq // tq, q_block, 0)


def mla_attention(q, kv, proj, *, batch, seq, tq=512, hp=4):
    T = q.shape[0]
    tq = min(tq, seq)
    kr_block = (MLA_Q_RANK + MLA_KV_RANK) // LANES
    return pl.pallas_call(
        functools.partial(_mla_attn_kernel, seq=seq, tq=tq, hp=hp, scale=(MLA_NOPE + MLA_ROPE) ** -0.5),
        out_shape=jax.ShapeDtypeStruct((T, MLA_HEADS * LANES), BF16), grid=(batch, MLA_HEADS // hp),
        in_specs=[pl.BlockSpec((seq, hp * MLA_QK_PAD), lambda b, g: (b, g)),
                  pl.BlockSpec((seq, hp * MLA_QK_PAD), lambda b, g: (b, g)),
                  pl.BlockSpec((seq, LANES), lambda b, g: (b, kr_block))],
        out_specs=pl.BlockSpec((seq, hp * LANES), lambda b, g: (b, g)),
        compiler_params=_params("parallel", "parallel"), name="mla_attention",
    )(q, kv, proj)


def _xa_kernel(q_ref, k_ref, v_ref, o_ref, *, scale):
    q = (q_ref[...].astype(F32) * scale).astype(BF16)
    s = _dot_nt(q, k_ref[...])
    p = jnp.exp(s - jnp.max(s, axis=1, keepdims=True))
    l = jnp.sum(p, axis=1, keepdims=True)
    o_ref[...] = (_dot(p.astype(BF16), v_ref[...]) / l).astype(o_ref.dtype)


def cross_attention(q, kv, *, batch, seq, mem_len, tq=1024):
    T, D = q.shape
    hd = D // XA_HEADS
    tq = min(tq, seq)
    nq = seq // tq
    return pl.pallas_call(
        functools.partial(_xa_kernel, scale=hd ** -0.5),
        out_shape=jax.ShapeDtypeStruct((T, D), BF16), grid=(batch, XA_HEADS, nq),
        in_specs=[pl.BlockSpec((tq, hd), lambda b, h, i: (b * nq + i, h)),
                  pl.BlockSpec((mem_len, hd), lambda b, h, i: (b, h)),
                  pl.BlockSpec((mem_len, hd), lambda b, h, i: (b, XA_HEADS + h))],
        out_specs=pl.BlockSpec((tq, hd), lambda b, h, i: (b * nq + i, h)),
        compiler_params=_params("parallel", "parallel", "parallel"), name="cross_attention",
    )(q, kv, kv)


def _s5_kernel(u_ref, bmat_ref, are_ref, aim_ref, cmat_ref, d_ref, wglu_ref, bglu_ref, o_ref,
               bu_ref, h_ref, *, batch, tc, ns, lane_chunk):
    @pl.when(pl.program_id(0) == 0)
    def _():
        h_ref[...] = jnp.zeros_like(h_ref)

    u = u_ref[...]
    bu_ref[...] = _dot(u, bmat_ref[...])
    for lc in range(ns // lane_chunk):
        re = slice(lc * lane_chunk, (lc + 1) * lane_chunk)
        im = slice(ns + lc * lane_chunk, ns + (lc + 1) * lane_chunk)
        ar = jnp.broadcast_to(are_ref[:, re], (batch, lane_chunk))
        ai = jnp.broadcast_to(aim_ref[:, re], (batch, lane_chunk))

        def step(t, carry, re=re, im=im, ar=ar, ai=ai):
            hr, hi = carry
            rows = pl.ds(pl.multiple_of(t * batch, batch), batch)
            nr = ar * hr - ai * hi + bu_ref[rows, re]
            ni = ar * hi + ai * hr + bu_ref[rows, im]
            bu_ref[rows, re] = nr
            bu_ref[rows, im] = ni
            return nr, ni

        hr, hi = lax.fori_loop(0, tc, step, (h_ref[:, re], h_ref[:, im]), unroll=4)
        h_ref[:, re] = hr
        h_ref[:, im] = hi
    y = _dot(bu_ref[...].astype(BF16), cmat_ref[...]) + d_ref[...] * u.astype(F32)
    g = _gelu(y)
    z = _dot(g.astype(BF16), wglu_ref[...]) + bglu_ref[...]
    o_ref[...] = (g / (1.0 + jnp.exp(-z))).astype(o_ref.dtype)


def s5_mixer(u_tm, bmat, a_re, a_im, cmat, d, w_glu, b_glu, *, batch, seq, tc=64):
    W = u_tm.shape[1]
    ns = a_re.shape[1]
    tc = min(tc, seq)
    rows = tc * batch
    full = lambda a: pl.BlockSpec(a.shape, lambda t: (0,) * a.ndim)
    args = (bmat, a_re, a_im, cmat, d, w_glu, b_glu)
    return pl.pallas_call(
        functools.partial(_s5_kernel, batch=batch, tc=tc, ns=ns, lane_chunk=min(512, ns)),
        out_shape=jax.ShapeDtypeStruct(u_tm.shape, BF16), grid=(seq // tc,),
        in_specs=[pl.BlockSpec((rows, W), lambda t: (t, 0))] + [full(a) for a in args],
        out_specs=pl.BlockSpec((rows, W), lambda t: (t, 0)),
        scratch_shapes=[pltpu.VMEM((rows, 2 * ns), F32), pltpu.VMEM((batch, 2 * ns), F32)],
        compiler_params=_params("arbitrary"), name="s5_mixer",
    )(u_tm, *args)


def _sort_network(n):
    def merge(lo, hi, r):
        step = r * 2
        if step < hi - lo:
            yield from merge(lo, hi, step)
            yield from merge(lo + r, hi, step)
            yield from ((i, i + r) for i in range(lo + r, hi - r, step))
        else:
            yield (lo, lo + r)

    def sort(lo, hi):
        if hi - lo >= 1:
            mid = lo + (hi - lo) // 2
            yield from sort(lo, mid)
            yield from sort(mid + 1, hi)
            yield from merge(lo, hi, 1)

    return list(sort(0, n - 1))


def _top16_tiles(x):
    K = PEER_TOPK
    return _top16_of_tiles([x[i * SUBLANES:(i + 1) * SUBLANES, :] for i in range(K)])


def _top16_of_tiles(t):
    K = PEER_TOPK
    t = list(t) + [jnp.full((SUBLANES, LANES), -jnp.inf, F32)] * (K - len(t))

    def cmpx(i, j):
        hi, lo = jnp.maximum(t[i], t[j]), jnp.minimum(t[i], t[j])
        t[i], t[j] = hi, lo

    for i, j in _sort_network(K):
        cmpx(i, j)
    for shift in (4, 2, 1):
        other = [pltpu.roll(v, shift, 0) for v in t]
        t = [jnp.maximum(t[i], other[K - 1 - i]) for i in range(K)]
        d = K // 2
        while d >= 1:
            for i in range(K):
                if i & d == 0:
                    cmpx(i, i + d)
            d //= 2
    return t


def _ranks_on_sublanes(tiles):
    sub = lax.broadcasted_iota(jnp.int32, (SUBLANES, LANES), 0)
    out = tiles[SUBLANES - 1]
    for r in range(SUBLANES - 2, -1, -1):
        out = jnp.where(sub == r, tiles[r], out)
    return out


def _peer_score_kernel(q_ref, keys_ref, s1_ref, s2_ref, a_ref, b_ref, tau_ref, *, tm, n_i):
    s1 = _dot_nt(keys_ref[0], q_ref[:, :PEER_HALF])
    s2 = _dot_nt(keys_ref[1], q_ref[:, PEER_HALF:])
    s2_ref[...] = s2
    for grp in range(PEER_N_KEYS // n_i):
        s1_ref[grp] = s1[grp * n_i:(grp + 1) * n_i, :]
    K = PEER_TOPK
    for c in range(tm // LANES):
        lanes = slice(c * LANES, (c + 1) * LANES)
        x1 = s1[:, lanes]
        x2 = s2[:, lanes]
        t1 = _top16_tiles(x1)
        t2 = _top16_tiles(x2)
        t2_lo, t2_hi = _ranks_on_sublanes(t2[:SUBLANES]), _ranks_on_sublanes(t2[SUBLANES:])
        cands = [t1[0] + t2_lo, t1[0] + t2_hi]
        cands += [t1[i] + t2_lo for i in range(1, SUBLANES)]
        cands += [_ranks_on_sublanes(t1[SUBLANES:]) + t2[0]]
        cand = jnp.concatenate(cands, axis=0)
        best = _top16_of_tiles(cands)
        tau = best[K - 1][0:1]
        m = best[0][0:1]
        z = jnp.sum(jnp.where(cand >= tau, jnp.exp(cand - m), 0.0), axis=0, keepdims=True)
        tau_ref[:, lanes] = tau
        a = jnp.exp(x1 - t1[0][0:1]) / z
        for grp in range(PEER_N_KEYS // n_i):
            a_ref[grp, :, lanes] = a[grp * n_i:(grp + 1) * n_i, :]
        b_ref[:, lanes] = jnp.exp(x2 - t2[0][0:1])


def peer_scores(q, keys, *, n_i, tm=512):
    T = q.shape[0]
    tm = min(tm, T)
    H = PEER_HEADS
    n_grp = PEER_N_KEYS // n_i
    big = jax.ShapeDtypeStruct((H, PEER_N_KEYS, T), F32)
    big_spec = pl.BlockSpec((None, PEER_N_KEYS, tm), lambda i, h: (h, 0, i))
    grp = jax.ShapeDtypeStruct((H, n_grp, n_i, T), F32)
    grp_spec = pl.BlockSpec((None, n_grp, n_i, tm), lambda i, h: (h, 0, 0, i))
    return pl.pallas_call(
        functools.partial(_peer_score_kernel, tm=tm, n_i=n_i),
        out_shape=[grp, big, grp, big, jax.ShapeDtypeStruct((H, 1, T), F32)],
        grid=(T // tm, H),
        in_specs=[pl.BlockSpec((tm, 2 * PEER_HALF), lambda i, h: (i, h)),
                  pl.BlockSpec((2, PEER_N_KEYS, PEER_HALF), lambda i, h: (h, 0, 0))],
        out_specs=[grp_spec, big_spec, grp_spec, big_spec,
                   pl.BlockSpec((None, 1, tm), lambda i, h: (h, 0, i))],
        compiler_params=_params("parallel", "parallel"), name="peer_scores",
    )(q, keys)


def _gate_group(s1_ref, s2_ref, a_ref, b_ref, tau_ref, c, jb, n_i):
    heads = range(PEER_HEADS)
    lanes = slice(c * LANES, (c + 1) * LANES)
    rows = slice(jb * SUBLANES, (jb + 1) * SUBLANES)
    tau = [tau_ref[h, :, lanes] for h in heads]
    s2 = [s2_ref[h, rows, lanes] for h in heads]
    b = [b_ref[h, rows, lanes] for h in heads]
    out = []
    for ii in range(n_i):
        terms = [jnp.where(s1_ref[h, ii:ii + 1, lanes] + s2[h] >= tau[h],
                           a_ref[h, ii:ii + 1, lanes] * b[h], 0.0) for h in heads]
        while len(terms) > 1:
            terms = [x + y for x, y in zip(terms[0::2], terms[1::2])]
        out.append(terms[0])
    return out


def _exact_zero(tiles):
    while len(tiles) > 2:
        tiles = [x + y for x, y in zip(tiles[0::2], tiles[1::2])]
    if len(tiles) == 1:
        tiles = tiles * 2
    zero = [pltpu.bitcast(lax.shift_right_logical(pltpu.bitcast(t, jnp.uint32), jnp.uint32(32)), F32)
            for t in tiles]
    return jnp.concatenate(zero, axis=0).astype(BF16)


def _anchor(x, zeros):
    pack = 2 * SUBLANES
    rows_out = []
    for r in range(x.shape[0] // pack):
        row = x[r * pack:(r + 1) * pack, :]
        hits = sorted(((l0, z) for (rr, l0), z in zeros.items() if rr == r), key=lambda t: t[0])
        if hits:
            parts, pos = [], 0
            for l0, z in hits:
                if l0 > pos:
                    parts.append(row[:, pos:l0])
                parts.append(row[:, l0:l0 + LANES] + z)
                pos = l0 + LANES
            if pos < x.shape[1]:
                parts.append(row[:, pos:])
            row = jnp.concatenate(parts, axis=1)
        rows_out.append(row)
    return jnp.concatenate(rows_out, axis=0)


def _peer_mix_kernel(h_ref, u_ref, vt_ref, s1_ref, s1n_ref, s2_ref, a_ref, an_ref, b_ref, tau_ref, x_ref,
                     g_ref, o_ref, acc_ref, ht_ref, gs_ref, *, tm, te, final_norm):
    j = pl.program_id(1)
    n_i = te // PEER_N_KEYS
    n_c = tm // LANES
    n_jb = PEER_N_KEYS // SUBLANES
    c_lo = n_c // 2
    pack = 2 * SUBLANES

    def gates_lo(s1r, ar):
        out = []
        for c in range(c_lo):
            for jb in range(n_jb):
                tiles = _gate_group(s1r, s2_ref, ar, b_ref, tau_ref, c, jb, n_i)
                for ii, t in enumerate(tiles):
                    r0 = ii * PEER_N_KEYS + jb * SUBLANES
                    gs_ref[r0:r0 + SUBLANES, c * LANES:(c + 1) * LANES] = t
                out.append(tiles)
        return out

    @pl.when(j == 0)
    def _():
        acc_ref[...] = jnp.zeros_like(acc_ref)
        ht_ref[...] = h_ref[...].T
        gates_lo(s1_ref, a_ref)

    g_lo = gs_ref[...]
    u = u_ref[...]
    k_blocks = u.shape[1] // MXU_DEPTH
    n_r = te // pack
    groups = [(c, jb) for c in range(c_lo, n_c) for jb in range(n_jb)]
    g = [[[None] * n_c for _ in range(n_jb)] for _ in range(n_i)]
    zeros = {}
    for q, (c, jb) in enumerate(groups):
        tiles = _gate_group(s1_ref, s2_ref, a_ref, b_ref, tau_ref, c, jb, n_i)
        for ii, t in enumerate(tiles):
            g[ii][jb][c] = t
        r = (q * n_r) // len(groups)
        zeros[(r, ((r * k_blocks) // n_r) * MXU_DEPTH)] = _exact_zero(tiles)
    st = _dot(_anchor(u, zeros), ht_ref[...])
    g_hi = jnp.concatenate([jnp.concatenate(g[ii][jb][c_lo:], axis=1)
                            for ii in range(n_i) for jb in range(n_jb)], axis=0)
    wt = (jnp.concatenate([g_lo, g_hi], axis=1) * _gelu(st)).astype(BF16)

    vt = vt_ref[...]
    n_rv = vt.shape[0] // pack
    kb_v = te // MXU_DEPTH
    nxt = gates_lo(s1n_ref, an_ref)
    zeros = {}
    for q, tiles in enumerate(nxt):
        pos = (q * n_rv * kb_v) // len(nxt)
        zeros[(pos % n_rv, (pos // n_rv) * MXU_DEPTH)] = _exact_zero(tiles)
    acc_ref[...] += _dot(_anchor(vt, zeros), wt)

    @pl.when(j == pl.num_programs(1) - 1)
    def _():
        y = x_ref[...] + acc_ref[...].T
        if final_norm:
            ms = jnp.mean(y * y, axis=-1, keepdims=True)
            y = y * lax.rsqrt(ms + NORM_EPS) * g_ref[...]
        o_ref[...] = y


def peer_mix(h, u, vt, s1, s2, a, b, tau, x, g_final, *, final_norm, tm=512):
    T, D = x.shape
    E = u.shape[0]
    tm = min(tm, T)
    H = PEER_HEADS
    tok = pl.BlockSpec((tm, D), lambda i, j: (i, 0))
    n_i = s1.shape[2]
    te = n_i * PEER_N_KEYS
    ne = E // te
    sc = pl.BlockSpec((H, PEER_N_KEYS, tm), lambda i, j: (0, 0, i))
    grp = pl.BlockSpec((H, None, n_i, tm), lambda i, j: (0, j, 0, i))
    grp_next = pl.BlockSpec((H, None, n_i, tm), lambda i, j: (0, jnp.minimum(j + 1, ne - 1), 0, i))
    return pl.pallas_call(
        functools.partial(_peer_mix_kernel, tm=tm, te=te, final_norm=final_norm),
        out_shape=jax.ShapeDtypeStruct((T, D), F32), grid=(T // tm, ne),
        in_specs=[tok,
                  pl.BlockSpec((te, D), lambda i, j: (j, 0)),
                  pl.BlockSpec((None, D, te), lambda i, j: (j, 0, 0)),
                  grp, grp_next, sc, grp, grp_next, sc,
                  pl.BlockSpec((H, 1, tm), lambda i, j: (0, 0, i)),
                  tok,
                  pl.BlockSpec((1, D), lambda i, j: (0, 0))],
        out_specs=tok,
        scratch_shapes=[pltpu.VMEM((D, tm), F32), pltpu.VMEM((D, tm), BF16),
                        pltpu.VMEM((te, tm // 2), F32)],
        compiler_params=_params("arbitrary", "arbitrary"), name="peer_mix",
    )(h, u, vt, s1, s1, s2, a, a, b, tau, x, g_final.reshape(1, D).astype(F32))


def _rope_tables(positions):
    half = DIFF_HEAD_DIM // 2
    inv_freq = ROPE_THETA ** (-jnp.arange(half, dtype=F32) * 2.0 / DIFF_HEAD_DIM)
    ang = positions.astype(F32).reshape(-1, 1) * inv_freq
    cos, sin = jnp.cos(ang), jnp.sin(ang)
    return jnp.tile(cos, (1, 4)), jnp.concatenate([-sin, sin, -sin, sin], axis=1)


def _s5_params(a_re, a_im, log_step, b_re, b_im, c_re, c_im):
    G, N = a_re.shape
    P = b_re.shape[-1]
    step = jnp.exp(log_step.astype(F32))[:, None]
    mag = jnp.exp(a_re * step)
    ab_re = mag * jnp.cos(a_im * step)
    ab_im = mag * jnp.sin(a_im * step)
    den = a_re * a_re + a_im * a_im
    num_re = ab_re - 1.0
    f_re = (num_re * a_re + ab_im * a_im) / den
    f_im = (ab_im * a_re - num_re * a_im) / den
    bb_re = f_re[..., None] * b_re - f_im[..., None] * b_im
    bb_im = f_re[..., None] * b_im + f_im[..., None] * b_re
    eye = jnp.eye(G, dtype=F32)
    blk_in = lambda m: jnp.einsum('gnp,gh->gphn', m, eye).reshape(G * P, G * N)
    blk_out = lambda m: jnp.einsum('gpn,gh->gnhp', m, eye).reshape(G * N, G * P)
    bmat = jnp.concatenate([blk_in(bb_re), blk_in(bb_im)], axis=1).astype(BF16)
    cmat = jnp.concatenate([blk_out(c_re), -blk_out(c_im)], axis=0).astype(BF16)
    return bmat, ab_re.reshape(1, G * N), ab_im.reshape(1, G * N), cmat


def ab_block(x, i, layer, cos, sin, ab_norm, ab_w_in, ab_w_out, diff_lq1, diff_lk1, diff_lq2, diff_lk2,
             diff_subln, s5_a_re, s5_a_im, s5_log_step, s5_b_re, s5_b_im, s5_c_re, s5_c_im, s5_d,
             s5_w_glu, s5_b_glu, *, batch, seq):
    T = x.shape[0]
    diff_w = DIFF_HEADS * 2 * DIFF_HEAD_DIM
    s5_w = ab_w_in.shape[2] - 3 * diff_w
    lam_init = 0.8 - 0.6 * math.exp(-0.3 * layer)
    w_in = ab_w_in[i].astype(BF16)
    qk = norm_matmul(x, ab_norm[i], w_in[:, :2 * diff_w], tm=TM_RESIDENT, tn=2 * diff_w, out_dtype=BF16,
                     rope=(0, 1, (True,) * (2 * diff_w // LANES)), cos=cos, sin=sin, name="ab_in_qk")
    vu = norm_matmul(x, ab_norm[i], w_in[:, 2 * diff_w:], tm=TM_RESIDENT, tn=diff_w + s5_w, out_dtype=BF16,
                     name="ab_in_vu")
    a_out = diff_attention(qk, vu, diff_lq1[i], diff_lk1[i], diff_lq2[i], diff_lk2[i], diff_subln[i],
                           batch=batch, seq=seq, lam_init=lam_init)
    bmat, a_re, a_im, cmat = _s5_params(s5_a_re[i], s5_a_im[i], s5_log_step[i], s5_b_re[i], s5_b_im[i],
                                        s5_c_re[i], s5_c_im[i])
    u_tm = vu[:, diff_w:].reshape(batch, seq, s5_w).transpose(1, 0, 2).reshape(T, s5_w)
    b_tm = s5_mixer(u_tm, bmat, a_re, a_im, cmat, s5_d[i].reshape(1, s5_w).astype(F32),
                    s5_w_glu[i].astype(BF16), s5_b_glu[i].reshape(1, s5_w).astype(F32),
                    batch=batch, seq=seq)
    b_out = b_tm.reshape(seq, batch, s5_w).transpose(1, 0, 2).reshape(T, s5_w)
    w_out = ab_w_out[i].astype(BF16)
    return matmul_residual([(a_out, w_out[:diff_w]), (b_out, w_out[diff_w:])], x, tm=TM_RESIDENT,
                           tn=x.shape[1], name="ab_out")


def mla_block(x, i, cos, sin, mla_norm, mla_w_in, mla_q_norm, mla_kv_norm, mla_w_uq, mla_w_ukv, mla_w_o,
              *, batch, seq):
    in_w = mla_w_in.shape[2]
    in_pad = (MLA_Q_RANK + MLA_KV_RANK) + LANES
    w_in = jnp.pad(mla_w_in[i], ((0, 0), (0, in_pad - in_w))).astype(BF16)
    proj = norm_matmul(x, mla_norm[i], w_in, tm=TM_RESIDENT, tn=in_pad, out_dtype=BF16,
                       rope=(0, 1, (False,) * (in_pad // LANES - 1) + (True,)), cos=cos, sin=sin, name="mla_in")
    w_uq = mla_w_uq[i].reshape(MLA_Q_RANK, MLA_HEADS, MLA_NOPE + MLA_ROPE)
    w_uq = jnp.pad(w_uq, ((0, 0), (0, 0), (0, MLA_QK_PAD - MLA_NOPE - MLA_ROPE)))
    w_uq = w_uq.reshape(MLA_Q_RANK, MLA_HEADS * MLA_QK_PAD).astype(BF16)
    q = norm_matmul(proj, mla_q_norm[i], w_uq, tm=TM_RESIDENT, tn=w_uq.shape[1], out_dtype=BF16, x_col_block=0,
                    rope=(0, 1, (False, True) * MLA_HEADS), cos=cos, sin=sin, name="mla_q")
    kv = norm_matmul(proj, mla_kv_norm[i], mla_w_ukv[i].astype(BF16), tm=TM_RESIDENT, tn=mla_w_ukv.shape[2],
                     out_dtype=BF16, x_col_block=MLA_Q_RANK // MLA_KV_RANK, name="mla_kv")
    o = mla_attention(q, kv, proj, batch=batch, seq=seq)
    return matmul_residual([(o, mla_w_o[i].astype(BF16))], x, tm=TM_RESIDENT, tn=x.shape[1], name="mla_out")


def xa_block(x, mem, layer, xa_norm, xa_mem_norm, xa_w_q, xa_w_kv, xa_w_o, *, batch, seq):
    mem_len = mem.shape[0] // batch
    q = norm_matmul(x, xa_norm[layer], xa_w_q[layer].astype(BF16), tm=TM_RESIDENT, tn=x.shape[1],
                    out_dtype=BF16, name="xa_q")
    kv = norm_matmul(mem, xa_mem_norm[layer], xa_w_kv[layer].astype(BF16), tm=TM_STREAMED, tn=TN_STREAMED,
                     out_dtype=BF16, name="xa_kv")
    o = cross_attention(q, kv, batch=batch, seq=seq, mem_len=mem_len)
    return matmul_residual([(o, xa_w_o[layer].astype(BF16))], x, tm=TM_RESIDENT, tn=x.shape[1], name="xa_out")


def peer_block(x, layer, ffn_norm, peer_w_query, peer_sub_keys, peer_u, peer_v, final_norm, *, last):
    pq, hn = norm_matmul(x, ffn_norm[layer], peer_w_query[layer].astype(BF16), tm=TM_RESIDENT,
                         tn=peer_w_query.shape[2], out_dtype=BF16, emit_xn=True, name="peer_q")
    keys = peer_sub_keys[layer].reshape(2 * PEER_HEADS, PEER_N_KEYS, PEER_HALF).astype(BF16)
    s1, s2, a, b, tau = peer_scores(pq, keys, n_i=PEER_TILE_KEYS)
    te = PEER_TILE_KEYS * PEER_N_KEYS
    vt = peer_v[layer].astype(BF16).reshape(-1, te, x.shape[1]).transpose(0, 2, 1)
    return peer_mix(hn, peer_u[layer].astype(BF16), vt, s1, s2, a, b, tau,
                    x, final_norm, final_norm=last)


def kernel(x, mem, positions, ab_norm, ab_w_in, ab_w_out, diff_lq1, diff_lk1, diff_lq2, diff_lk2, diff_subln, s5_a_re, s5_a_im, s5_log_step, s5_b_re, s5_b_im, s5_c_re, s5_c_im, s5_d, s5_w_glu, s5_b_glu, mla_norm, mla_w_in, mla_q_norm, mla_kv_norm, mla_w_uq, mla_w_ukv, mla_w_o, xa_norm, xa_mem_norm, xa_w_q, xa_w_kv, xa_w_o, ffn_norm, peer_w_query, peer_sub_keys, peer_u, peer_v, final_norm):
    batch, seq, D = x.shape
    depth = xa_norm.shape[0]
    x = x.reshape(batch * seq, D)
    mem = mem.reshape(-1, D)
    cos, sin = _rope_tables(positions)
    for layer in range(depth):
        i = layer // 2
        if layer % 2 == 0:
            x = ab_block(x, i, layer, cos, sin, ab_norm, ab_w_in, ab_w_out, diff_lq1, diff_lk1, diff_lq2,
                         diff_lk2, diff_subln, s5_a_re, s5_a_im, s5_log_step, s5_b_re, s5_b_im, s5_c_re,
                         s5_c_im, s5_d, s5_w_glu, s5_b_glu, batch=batch, seq=seq)
        else:
            x = mla_block(x, i, cos, sin, mla_norm, mla_w_in, mla_q_norm, mla_kv_norm, mla_w_uq, mla_w_ukv,
                          mla_w_o, batch=batch, seq=seq)
        x = xa_block(x, mem, layer, xa_norm, xa_mem_norm, xa_w_q, xa_w_kv, xa_w_o, batch=batch, seq=seq)
        x = peer_block(x, layer, ffn_norm, peer_w_query, peer_sub_keys, peer_u, peer_v, final_norm,
                       last=(layer == depth - 1))
    return x.reshape(batch, seq, D)
```
